```python
import math
import jax, jax.numpy as jnp
from jax import lax
import numpy as np

D_MODEL = 1024
BATCH = 8
SEQ = 2048
DEPTH = 1

HEAD_DIM = 128
DN_HEADS = D_MODEL // HEAD_DIM
DN_WIDTH = DN_HEADS * HEAD_DIM
POOL_WINDOWS = (2, 4, 8, 16)
POOL_GROUPS = len(POOL_WINDOWS)
POOL_WIDTH = D_MODEL // 2
POOL_GROUP_DIM = POOL_WIDTH // POOL_GROUPS
MEM_LEN = 256
MEM_HEADS = 4
MEM_WIDTH = D_MODEL // 2
MEM_HEAD_DIM = MEM_WIDTH // MEM_HEADS
CONV_WIDTH = 4
CHUNK = 64
N_BRANCH = 3
EPS = 1e-6

IN_SPLITS = (POOL_WIDTH, POOL_WIDTH, DN_WIDTH, DN_WIDTH, DN_WIDTH, DN_HEADS, DN_HEADS,
             DN_WIDTH, MEM_WIDTH, MEM_WIDTH, N_BRANCH * D_MODEL)
IN_WIDTH = int(sum(IN_SPLITS))
IN_OFFSETS = [int(o) for o in np.cumsum(IN_SPLITS)[:-1]]

kernel_name = "hybrid_pool_deltanet_memxattn_gated_merge"


def rms_norm(x, w):
    xf = x.astype(jnp.float32)
    y = xf * lax.rsqrt(jnp.mean(xf * xf, axis=-1, keepdims=True) + EPS)
    return (y * w.astype(jnp.float32)).astype(x.dtype)


def l2norm(x):
    return x * lax.rsqrt(jnp.sum(x * x, axis=-1, keepdims=True) + EPS)


def pool_mixer(u, mix_w, scale):
    B, S, _ = u.shape
    uf = u.astype(jnp.float32)
    c = jnp.cumsum(uf, axis=1)
    t = jnp.arange(1, S + 1, dtype=jnp.float32)[None, :, None]
    outs = []
    for g, w in enumerate(POOL_WINDOWS):
        sl = slice(g * POOL_GROUP_DIM, (g + 1) * POOL_GROUP_DIM)
        cg = c[..., sl]
        prev = jnp.pad(cg, ((0, 0), (w, 0), (0, 0)))[:, :S]
        mean = (cg - prev) / jnp.minimum(t, float(w))
        outs.append(mean - uf[..., sl])
    p = jnp.stack(outs, axis=2).astype(u.dtype)
    m = jnp.einsum('bsgc,gcd->bsgd', p, mix_w).reshape(B, S, POOL_WIDTH)
    return m * scale


def causal_dwconv(x, w):
    K, C = w.shape
    return lax.conv_general_dilated(x, w[:, None, :].astype(x.dtype), window_strides=(1,),
                                    padding=[(K - 1, 0)],
                                    dimension_numbers=('NWC', 'WIO', 'NWC'),
                                    feature_group_count=C)


def gated_delta_rule(q, k, v, g, beta):
    B, S, H, Dk = q.shape
    Dv = v.shape[-1]
    N = S // CHUNK

    def chunks(t):
        return t.reshape(B, N, CHUNK, H, -1).transpose(0, 3, 1, 2, 4)

    q = chunks(l2norm(q) * (Dk ** -0.5))
    k = chunks(l2norm(k))
    v = chunks(v)
    g = g.reshape(B, N, CHUNK, H).transpose(0, 3, 1, 2)
    beta = beta.reshape(B, N, CHUNK, H).transpose(0, 3, 1, 2)
    gc = jnp.cumsum(g, axis=-1)

    tril = jnp.tril(jnp.ones((CHUNK, CHUNK), dtype=bool))
    stril = jnp.tril(jnp.ones((CHUNK, CHUNK), dtype=bool), k=-1)
    diff = gc[..., :, None] - gc[..., None, :]
    decay = jnp.where(tril, jnp.exp(jnp.where(tril, diff, 0.0)), 0.0)

    kb = k * beta[..., None]
    A = jnp.where(stril, jnp.einsum('bhnid,bhnjd->bhnij', kb, k) * decay, 0.0)
    eye = jnp.eye(CHUNK, dtype=A.dtype)
    T = lax.linalg.triangular_solve(eye + A, jnp.broadcast_to(eye, A.shape),
                                    left_side=True, lower=True, unit_diagonal=True)
    u = jnp.einsum('bhnij,bhnjd->bhnid', T, v * beta[..., None])
    w = jnp.einsum('bhnij,bhnjd->bhnid', T, kb * jnp.exp(gc)[..., None])
    a_qk = jnp.where(tril, jnp.einsum('bhnid,bhnjd->bhnij', q, k) * decay, 0.0)
    qg = q * jnp.exp(gc)[..., None]
    kd = k * jnp.exp(gc[..., -1:] - gc)[..., None]
    glast = jnp.exp(gc[..., -1])

    xs = tuple(jnp.moveaxis(t, 2, 0) for t in (u, w, a_qk, qg, kd, glast))

    def step(state, inp):
        u_n, w_n, a_n, qg_n, kd_n, gl_n = inp
        v_new = u_n - jnp.einsum('bhck,bhkv->bhcv', w_n, state)
        o = jnp.einsum('bhck,bhkv->bhcv', qg_n, state) + jnp.einsum('bhcj,bhjv->bhcv', a_n, v_new)
        state = state * gl_n[..., None, None] + jnp.einsum('bhck,bhcv->bhkv', kd_n, v_new)
        return state, o

    s0 = jnp.zeros((B, H, Dk, Dv), dtype=jnp.float32)
    _, o = lax.scan(step, s0, xs)
    return o.transpose(1, 0, 3, 2, 4).reshape(B, S, H, Dv)


def memory_attention(qm, mem_n, w_kv):
    B, S, _ = qm.shape
    kv = mem_n @ w_kv
    km, vm = jnp.split(kv, 2, axis=-1)
    q = qm.reshape(B, S, MEM_HEADS, MEM_HEAD_DIM)
    km = km.reshape(B, -1, MEM_HEADS, MEM_HEAD_DIM)
    vm = vm.reshape(B, -1, MEM_HEADS, MEM_HEAD_DIM)
    s = jnp.einsum('bshd,bmhd->bhsm', q, km).astype(jnp.float32) * (MEM_HEAD_DIM ** -0.5)
    p = jax.nn.softmax(s, axis=-1).astype(vm.dtype)
    return jnp.einsum('bhsm,bmhd->bshd', p, vm).reshape(B, S, MEM_WIDTH)


def hybrid_layer(x, mem, pre_norm_w, mem_norm_w, w_in, conv_w, a_log, dt_bias, dn_norm_w,
                 pool_mix_w, pool_scale, w_mem_kv, w_proj_pool, w_proj_delta, w_proj_mem,
                 w_out, post_norm_w):
    B, S, D = x.shape
    h = rms_norm(x, pre_norm_w)
    proj = h @ w_in
    (xa, za, qd, kd, vd, a_raw, b_raw, zd, qm, zm, gate_raw) = jnp.split(proj, IN_OFFSETS, axis=-1)

    ya = pool_mixer(xa, pool_mix_w, pool_scale) * jax.nn.silu(za)

    qkv = jax.nn.silu(causal_dwconv(jnp.concatenate([qd, kd, vd], axis=-1), conv_w))
    qd, kd, vd = jnp.split(qkv, 3, axis=-1)
    shp = (B, S, DN_HEADS, HEAD_DIM)
    g = -jnp.exp(a_log.astype(jnp.float32)) * jax.nn.softplus(
        a_raw.astype(jnp.float32) + dt_bias.astype(jnp.float32))
    beta = jax.nn.sigmoid(b_raw.astype(jnp.float32))
    o = gated_delta_rule(qd.reshape(shp).astype(jnp.float32), kd.reshape(shp).astype(jnp.float32),
                         vd.reshape(shp).astype(jnp.float32), g, beta)
    yb = rms_norm(o, dn_norm_w).reshape(B, S, DN_WIDTH).astype(x.dtype) * jax.nn.silu(zd)

    yc = memory_attention(qm, rms_norm(mem, mem_norm_w), w_mem_kv) * jax.nn.silu(zm)

    gates = jax.nn.sigmoid(gate_raw).reshape(B, S, N_BRANCH, D)
    y = (gates[:, :, 0] * (ya @ w_proj_pool)
         + gates[:, :, 1] * (yb @ w_proj_delta)
         + gates[:, :, 2] * (yc @ w_proj_mem))
    out = y @ w_out
    return x + rms_norm(out, post_norm_w)


def setup_inputs(seed: int = 0) -> dict:
    key = jax.random.key(seed)
    ks = jax.random.split(key, 20)
    L, D = DEPTH, D_MODEL
    f32 = jnp.float32

    def nrm(k, shape, fan_in):
        return jax.random.normal(k, shape, f32) * (fan_in ** -0.5)

    dt = jnp.exp(jax.random.uniform(ks[8], (L, DN_HEADS), f32, math.log(1e-3), math.log(1e-1)))
    return {
        "x": jax.random.normal(ks[0], (BATCH, SEQ, D), f32),
        "mem": jax.random.normal(ks[1], (BATCH, MEM_LEN, D), f32),
        "pre_norm_w": 1.0 + 0.05 * jax.random.normal(ks[2], (L, D), f32),
        "mem_norm_w": 1.0 + 0.05 * jax.random.normal(ks[3], (L, D), f32),
        "w_in": nrm(ks[4], (L, D, IN_WIDTH), D),
        "conv_w": nrm(ks[5], (L, CONV_WIDTH, 3 * DN_WIDTH), CONV_WIDTH),
        "a_log": jnp.log(jax.random.uniform(ks[6], (L, DN_HEADS), f32, 1.0, 16.0)),
        "dt_bias": dt + jnp.log(-jnp.expm1(-dt)),
        "dn_norm_w": 1.0 + 0.05 * jax.random.normal(ks[7], (L, HEAD_DIM), f32),
        "pool_mix_w": nrm(ks[9], (L, POOL_GROUPS, POOL_GROUP_DIM, POOL_GROUP_DIM), POOL_GROUP_DIM),
        "pool_scale": 1.0 + 0.1 * jax.random.normal(ks[10], (L, POOL_WIDTH), f32),
        "w_mem_kv": nrm(ks[11], (L, D, 2 * MEM_WIDTH), D),
        "w_proj_pool": nrm(ks[12], (L, POOL_WIDTH, D), POOL_WIDTH),
        "w_proj_delta": nrm(ks[13], (L, DN_WIDTH, D), DN_WIDTH),
        "w_proj_mem": nrm(ks[14], (L, MEM_WIDTH, D), MEM_WIDTH),
        "w_out": nrm(ks[15], (L, D, D), D),
        "post_norm_w": 1.0 + 0.05 * jax.random.normal(ks[16], (L, D), f32),
    }


def reference(x, mem, pre_norm_w, mem_norm_w, w_in, conv_w, a_log, dt_bias, dn_norm_w,
              pool_mix_w, pool_scale, w_mem_kv, w_proj_pool, w_proj_delta, w_proj_mem,
              w_out, post_norm_w):
    for l in range(DEPTH):
        x = hybrid_layer(x, mem, pre_norm_w[l], mem_norm_w[l], w_in[l], conv_w[l], a_log[l],
                         dt_bias[l], dn_norm_w[l], pool_mix_w[l], pool_scale[l], w_mem_kv[l],
                         w_proj_pool[l], w_proj_delta[l], w_proj_mem[l], w_out[l], post_norm_w[l])
    return x
```

```python
import jax
import jax.numpy as jnp
from jax import lax
from jax.experimental import pallas as pl
from jax.experimental.pallas import tpu as pltpu

F32 = jnp.float32
BF16 = jnp.bfloat16

D_MODEL = 1024
HEAD_DIM = 128
DN_HEADS = D_MODEL // HEAD_DIM
DN_WIDTH = DN_HEADS * HEAD_DIM
POOL_WINDOWS = (2, 4, 8, 16)
POOL_GROUPS = len(POOL_WINDOWS)
POOL_WIDTH = D_MODEL // 2
POOL_GROUP_DIM = POOL_WIDTH // POOL_GROUPS
MEM_HEADS = 4
MEM_WIDTH = D_MODEL // 2
MEM_HEAD_DIM = MEM_WIDTH // MEM_HEADS
CONV_WIDTH = 4
CHUNK = 64
N_BRANCH = 3
EPS = 1e-6

OFF_XA = 0
OFF_ZA = OFF_XA + POOL_WIDTH
OFF_Q = OFF_ZA + POOL_WIDTH
OFF_K = OFF_Q + DN_WIDTH
OFF_V = OFF_K + DN_WIDTH
OFF_A = OFF_V + DN_WIDTH
OFF_B = OFF_A + DN_HEADS
OFF_ZD = OFF_B + DN_HEADS
OFF_QM = OFF_ZD + DN_WIDTH
OFF_ZM = OFF_QM + MEM_WIDTH
OFF_GATE = OFF_ZM + MEM_WIDTH

LANES = 128
SUBLANES = 8
SPAN = 2 * CHUNK
POOL_HALO = 16
CONV_HALO = SUBLANES
VMEM_LIMIT_BYTES = 56 * 1024 * 1024

TILE_LOCAL = 512
TILE_DELTA = 512
TILE_MERGE = 512


def _rms(x, w):
    return x * lax.rsqrt(jnp.mean(x * x, axis=-1, keepdims=True) + EPS) * w


def _sigmoid(x):
    return 1.0 / (1.0 + jnp.exp(-x))


def _silu(x):
    return x * _sigmoid(x)


def _dot(a, b):
    return jnp.dot(a, b, preferred_element_type=F32)


def _dot_nt(a, b):
    return lax.dot_general(a, b, (((1,), (1,)), ((), ())), preferred_element_type=F32)


def _mem_kv_kernel(mem_ref, nw_ref, wkt_ref, wv_ref, kt_ref, v_ref):
    mn = _rms(mem_ref[...], nw_ref[...]).astype(BF16)
    kt_ref[...] = _dot_nt(wkt_ref[...], mn).astype(BF16)
    v_ref[...] = _dot(mn, wv_ref[...]).astype(BF16)


def _mem_kv(mem, mem_norm_w, w_kt, w_v):
    b, m, d = mem.shape
    return pl.pallas_call(
        _mem_kv_kernel,
        grid=(b,),
        in_specs=[
            pl.BlockSpec((None, m, d), lambda i: (i, 0, 0)),
            pl.BlockSpec((1, d), lambda i: (0, 0)),
            pl.BlockSpec((MEM_WIDTH, d), lambda i: (0, 0)),
            pl.BlockSpec((d, MEM_WIDTH), lambda i: (0, 0)),
        ],
        out_specs=[
            pl.BlockSpec((None, MEM_WIDTH, m), lambda i: (i, 0, 0)),
            pl.BlockSpec((None, m, MEM_WIDTH), lambda i: (i, 0, 0)),
        ],
        out_shape=[
            jax.ShapeDtypeStruct((b, MEM_WIDTH, m), BF16),
            jax.ShapeDtypeStruct((b, m, MEM_WIDTH), BF16),
        ],
        compiler_params=pltpu.CompilerParams(
            dimension_semantics=("arbitrary",), vmem_limit_bytes=VMEM_LIMIT_BYTES),
        name="mem_kv",
    )(mem, mem_norm_w, w_kt, w_v)


def _local_kernel(x_ref, nw_ref, w_ref, mix_ref, scale_ref, kt_ref, v_ref, ya_ref, yc_ref,
                  halo_ref):
    tm = x_ref.shape[0]
    s = pl.program_id(1)

    @pl.when(s == 0)
    def _():
        halo_ref[...] = jnp.zeros_like(halo_ref)

    h = _rms(x_ref[...], nw_ref[...]).astype(BF16)
    proj = _dot(h, w_ref[...])
    xa = proj[:, 0:POOL_WIDTH]
    za = proj[:, POOL_WIDTH:2 * POOL_WIDTH]
    qm = proj[:, 2 * POOL_WIDTH:2 * POOL_WIDTH + MEM_WIDTH]
    zm = proj[:, 2 * POOL_WIDTH + MEM_WIDTH:]

    ext = jnp.concatenate([halo_ref[...], xa], axis=0)
    halo_ref[...] = xa[tm - POOL_HALO:, :]
    pos = (s * tm + 1 + lax.broadcasted_iota(jnp.int32, (tm, 1), 0)).astype(F32)
    mixed = []
    for g, window in enumerate(POOL_WINDOWS):
        cols = slice(g * POOL_GROUP_DIM, (g + 1) * POOL_GROUP_DIM)
        acc = ext[:, cols]
        width = 1
        while width < window:
            acc = acc + pltpu.roll(acc, width, 0)
            width *= 2
        mean = acc[POOL_HALO:, :] / jnp.minimum(pos, float(window))
        p = mean - xa[:, cols]
        mixed.append(_dot(p.astype(BF16), mix_ref[g]))
    m = jnp.concatenate(mixed, axis=1) * scale_ref[...]
    ya_ref[...] = (m * _silu(za)).astype(BF16)

    heads = []
    for hh in range(MEM_HEADS):
        cols = slice(hh * MEM_HEAD_DIM, (hh + 1) * MEM_HEAD_DIM)
        sc = _dot(qm[:, cols].astype(BF16), kt_ref[cols, :]) * (MEM_HEAD_DIM ** -0.5)
        e = jnp.exp(sc - jnp.max(sc, axis=-1, keepdims=True))
        o = _dot(e.astype(BF16), v_ref[:, cols])
        heads.append(o / jnp.sum(e, axis=-1, keepdims=True))
    yc_ref[...] = (jnp.concatenate(heads, axis=1) * _silu(zm)).astype(BF16)


def _local(x, pre_norm_w, w_loc, mix_w, pool_scale, kt, v):
    b, s, d = x.shape
    tm = TILE_LOCAL
    m = kt.shape[-1]
    const2 = lambda i, j: (0, 0)
    return pl.pallas_call(
        _local_kernel,
        grid=(b, s // tm),
        in_specs=[
            pl.BlockSpec((None, tm, d), lambda i, j: (i, j, 0)),
            pl.BlockSpec((1, d), const2),
            pl.BlockSpec(w_loc.shape, const2),
            pl.BlockSpec(mix_w.shape, lambda i, j: (0, 0, 0)),
            pl.BlockSpec((1, POOL_WIDTH), const2),
            pl.BlockSpec((None, MEM_WIDTH, m), lambda i, j: (i, 0, 0)),
            pl.BlockSpec((None, m, MEM_WIDTH), lambda i, j: (i, 0, 0)),
        ],
        out_specs=[
            pl.BlockSpec((None, tm, POOL_WIDTH), lambda i, j: (i, j, 0)),
            pl.BlockSpec((None, tm, MEM_WIDTH), lambda i, j: (i, j, 0)),
        ],
        out_shape=[
            jax.ShapeDtypeStruct((b, s, POOL_WIDTH), BF16),
            jax.ShapeDtypeStruct((b, s, MEM_WIDTH), BF16),
        ],
        scratch_shapes=[pltpu.VMEM((POOL_HALO, POOL_WIDTH), F32)],
        compiler_params=pltpu.CompilerParams(
            dimension_semantics=("arbitrary", "arbitrary"), vmem_limit_bytes=VMEM_LIMIT_BYTES),
        name="local_branches",
    )(x, pre_norm_w, w_loc, mix_w, pool_scale, kt, v)


def _delta_kernel(x_ref, nw_ref, wqkv_ref, wab_ref, wzd_ref, convw_ref, alog_ref, dtb_ref,
                  dnw_ref, yb_ref, qkv_buf, state_ref, zs_buf, gc_buf, beta_buf, gct_buf):
    tm = x_ref.shape[0]
    n_span = tm // SPAN
    s = pl.program_id(1)

    @pl.when(s == 0)
    def _():
        qkv_buf[0:CONV_HALO, :] = jnp.zeros((CONV_HALO, 3 * DN_WIDTH), F32)
        state_ref[...] = jnp.zeros_like(state_ref)

    @pl.when(s > 0)
    def _():
        qkv_buf[0:CONV_HALO, :] = qkv_buf[tm:tm + CONV_HALO, :]

    h = _rms(x_ref[...], nw_ref[...]).astype(BF16)
    qkv_buf[CONV_HALO:CONV_HALO + tm, :] = _dot(h, wqkv_ref[...])
    zs_buf[...] = _silu(_dot(h, wzd_ref[...]))

    ab = _dot(h, wab_ref[...])
    sp_in = ab + dtb_ref[...]
    softplus = jnp.maximum(sp_in, 0.0) + jnp.log1p(jnp.exp(-jnp.abs(sp_in)))
    g = -jnp.exp(alog_ref[...]) * softplus
    beta_buf[...] = _sigmoid(ab)

    row_in_chunk = lax.broadcasted_iota(jnp.int32, (tm, LANES), 0) % CHUNK
    gc = g
    shift = 1
    while shift < CHUNK:
        gc = gc + jnp.where(row_in_chunk >= shift, pltpu.roll(gc, shift, 0), 0.0)
        shift *= 2
    gc_buf[...] = gc
    for i in range(n_span):
        gct_buf[i] = gc[i * SPAN:(i + 1) * SPAN, :].T

    ri = lax.broadcasted_iota(jnp.int32, (SPAN, SPAN), 0)
    ci = lax.broadcasted_iota(jnp.int32, (SPAN, SPAN), 1)
    same_chunk = (ri // CHUNK) == (ci // CHUNK)
    tril = same_chunk & (ri >= ci)
    stril = same_chunk & (ri > ci)
    eye = (ri == ci).astype(F32)
    first_chunk_rows = lax.broadcasted_iota(jnp.int32, (SPAN, LANES), 0) < CHUNK

    def conv_silu(r0, c0):
        blk = qkv_buf[pl.ds(r0, SPAN + CONV_HALO), c0:c0 + HEAD_DIM]
        w = convw_ref[:, c0:c0 + HEAD_DIM]
        acc = blk * w[CONV_WIDTH - 1:CONV_WIDTH, :]
        for back in range(1, CONV_WIDTH):
            tap = CONV_WIDTH - 1 - back
            acc = acc + pltpu.roll(blk, back, 0) * w[tap:tap + 1, :]
        return _silu(acc[CONV_HALO:, :])

    def span_body(sp, carry):
        r0 = pl.multiple_of(sp * SPAN, SPAN)
        gc_s = gc_buf[pl.ds(r0, SPAN), :]
        beta_s = beta_buf[pl.ds(r0, SPAN), :]
        gct_s = gct_buf[sp]
        last0 = gc_s[CHUNK - 1:CHUNK, :]
        last1 = gc_s[SPAN - 1:SPAN, :]
        glast = jnp.where(first_chunk_rows, last0, last1)
        e_s = jnp.exp(gc_s)
        kdf_s = jnp.exp(glast - gc_s)
        eg0 = jnp.exp(last0)
        eg1 = jnp.exp(last1)

        for hd in range(DN_HEADS):
            q = conv_silu(r0, hd * HEAD_DIM)
            k = conv_silu(r0, DN_WIDTH + hd * HEAD_DIM)
            v = conv_silu(r0, 2 * DN_WIDTH + hd * HEAD_DIM)
            qn = q * lax.rsqrt(jnp.sum(q * q, axis=-1, keepdims=True) + EPS) * (HEAD_DIM ** -0.5)
            kn = k * lax.rsqrt(jnp.sum(k * k, axis=-1, keepdims=True) + EPS)
            bcol = beta_s[:, DN_HEADS + hd:DN_HEADS + hd + 1]
            ecol = e_s[:, hd:hd + 1]
            kdcol = kdf_s[:, hd:hd + 1]
            diff = gc_s[:, hd:hd + 1] - gct_s[hd:hd + 1, :]
            decay = jnp.where(tril, jnp.exp(jnp.where(tril, diff, 0.0)), 0.0)

            kb = kn * bcol
            kq = _dot_nt(jnp.concatenate([kb, qn], axis=0).astype(BF16), kn.astype(BF16))
            a = jnp.where(stril, kq[:SPAN, :] * decay, 0.0)
            a_qk = kq[SPAN:, :] * decay

            a_b = a.astype(BF16)
            t_inv = eye - a
            power = _dot(a_b, a_b)
            order = 2
            while order < CHUNK:
                power_b = power.astype(BF16)
                t_inv = t_inv + _dot(t_inv.astype(BF16), power_b)
                order *= 2
                if order < CHUNK:
                    power = _dot(power_b, power_b)

            rhs = jnp.concatenate([v * bcol, kb * ecol], axis=1).astype(BF16)
            uw = _dot(t_inv.astype(BF16), rhs)
            u = uw[:, :HEAD_DIM]
            w = uw[:, HEAD_DIM:]
            qg = qn * ecol
            kdt = (kn * kdcol).T.astype(BF16)

            st = state_ref[hd]
            zeros_chunk = jnp.zeros((CHUNK, HEAD_DIM), F32)
            v_new = []
            o_inter = []
            for c, eg in ((0, eg0), (1, eg1)):
                rows = slice(c * CHUNK, (c + 1) * CHUNK)
                wq = jnp.concatenate([w[rows, :], qg[rows, :]], axis=0).astype(BF16)
                r = _dot(wq, st.astype(BF16))
                vn = u[rows, :] - r[:CHUNK, :]
                o_inter.append(r[CHUNK:, :])
                v_new.append(vn)
                padded = [zeros_chunk, zeros_chunk]
                padded[c] = vn
                vn_pad = jnp.concatenate(padded, axis=0).astype(BF16)
                st = st * eg[:, hd:hd + 1] + _dot(kdt, vn_pad)
            state_ref[hd] = st

            o = jnp.concatenate(o_inter, axis=0) + _dot(
                a_qk.astype(BF16), jnp.concatenate(v_new, axis=0).astype(BF16))
            cols = slice(hd * HEAD_DIM, (hd + 1) * HEAD_DIM)
            y = _rms(o, dnw_ref[...]) * zs_buf[pl.ds(r0, SPAN), cols]
            yb_ref[pl.ds(r0, SPAN), cols] = y.astype(BF16)
        return carry

    lax.fori_loop(0, n_span, span_body, 0)


def _delta(x, pre_norm_w, w_qkv, w_ab, w_zd, conv_w, a_log, dt_bias, dn_norm_w):
    b, s, d = x.shape
    tm = TILE_DELTA
    const2 = lambda i, j: (0, 0)
    return pl.pallas_call(
        _delta_kernel,
        grid=(b, s // tm),
        in_specs=[
            pl.BlockSpec((None, tm, d), lambda i, j: (i, j, 0)),
            pl.BlockSpec((1, d), const2),
            pl.BlockSpec(w_qkv.shape, const2),
            pl.BlockSpec(w_ab.shape, const2),
            pl.BlockSpec(w_zd.shape, const2),
            pl.BlockSpec(conv_w.shape, const2),
            pl.BlockSpec((1, LANES), const2),
            pl.BlockSpec((1, LANES), const2),
            pl.BlockSpec((1, HEAD_DIM), const2),
        ],
        out_specs=pl.BlockSpec((None, tm, DN_WIDTH), lambda i, j: (i, j, 0)),
        out_shape=jax.ShapeDtypeStruct((b, s, DN_WIDTH), BF16),
        scratch_shapes=[
            pltpu.VMEM((CONV_HALO + tm, 3 * DN_WIDTH), F32),
            pltpu.VMEM((DN_HEADS, HEAD_DIM, HEAD_DIM), F32),
            pltpu.VMEM((tm, DN_WIDTH), F32),
            pltpu.VMEM((tm, LANES), F32),
            pltpu.VMEM((tm, LANES), F32),
            pltpu.VMEM((tm // SPAN, LANES, SPAN), F32),
        ],
        compiler_params=pltpu.CompilerParams(
            dimension_semantics=("arbitrary", "arbitrary"), vmem_limit_bytes=VMEM_LIMIT_BYTES),
        name="delta_branch",
    )(x, pre_norm_w, w_qkv, w_ab, w_zd, conv_w, a_log, dt_bias, dn_norm_w)


def _merge_kernel(x_ref, nw_ref, wg_ref, ya_ref, yb_ref, yc_ref, wpa_ref, wpb_ref, wpc_ref,
                  wout_ref, pw_ref, o_ref):
    x = x_ref[...]
    h = _rms(x, nw_ref[...]).astype(BF16)
    y = None
    for br, (y_ref, w_ref) in enumerate(((ya_ref, wpa_ref), (yb_ref, wpb_ref), (yc_ref, wpc_ref))):
        gate = _sigmoid(_dot(h, wg_ref[:, br * D_MODEL:(br + 1) * D_MODEL]))
        term = gate * _dot(y_ref[...], w_ref[...])
        y = term if y is None else y + term
    out = _dot(y.astype(BF16), wout_ref[...])
    o_ref[...] = x + _rms(out, pw_ref[...])


def _merge(x2, pre_norm_w, w_gate, ya, yb, yc, w_pa, w_pb, w_pc, w_out, post_norm_w):
    n, d = x2.shape
    tm = TILE_MERGE
    const = lambda i: (0, 0)
    row = lambda i: (i, 0)
    return pl.pallas_call(
        _merge_kernel,
        grid=(n // tm,),
        in_specs=[
            pl.BlockSpec((tm, d), row),
            pl.BlockSpec((1, d), const),
            pl.BlockSpec(w_gate.shape, const),
            pl.BlockSpec((tm, POOL_WIDTH), row),
            pl.BlockSpec((tm, DN_WIDTH), row),
            pl.BlockSpec((tm, MEM_WIDTH), row),
            pl.BlockSpec(w_pa.shape, const),
            pl.BlockSpec(w_pb.shape, const),
            pl.BlockSpec(w_pc.shape, const),
            pl.BlockSpec(w_out.shape, const),
            pl.BlockSpec((1, d), const),
        ],
        out_specs=pl.BlockSpec((tm, d), row),
        out_shape=jax.ShapeDtypeStruct((n, d), F32),
        compiler_params=pltpu.CompilerParams(
            dimension_semantics=("arbitrary",), vmem_limit_bytes=VMEM_LIMIT_BYTES),
        name="merge_out",
    )(x2, pre_norm_w, w_gate, ya, yb, yc, w_pa, w_pb, w_pc, w_out, post_norm_w)


def _layer(x, mem, pre_norm_w, mem_norm_w, w_in, conv_w, a_log, dt_bias, dn_norm_w, pool_mix_w,
           pool_scale, w_mem_kv, w_proj_pool, w_proj_delta, w_proj_mem, w_out, post_norm_w):
    b, s, d = x.shape
    assert d == D_MODEL and s % TILE_LOCAL == 0 and s % TILE_DELTA == 0
    assert (b * s) % TILE_MERGE == 0 and TILE_DELTA % SPAN == 0
    row = lambda v: v.reshape(1, -1).astype(F32)
    pad_lanes = lambda v: jnp.pad(v.astype(F32), (0, LANES - v.shape[0])).reshape(1, LANES)

    w_in_b = w_in.astype(BF16)
    w_loc = jnp.concatenate(
        [w_in_b[:, OFF_XA:OFF_Q], w_in_b[:, OFF_QM:OFF_GATE]], axis=1)
    w_qkv = w_in_b[:, OFF_Q:OFF_A]
    w_ab = jnp.pad(w_in_b[:, OFF_A:OFF_ZD], ((0, 0), (0, LANES - 2 * DN_HEADS)))
    w_zd = w_in_b[:, OFF_ZD:OFF_QM]
    w_gate = w_in_b[:, OFF_GATE:]

    kt, v = _mem_kv(mem, row(mem_norm_w), w_mem_kv[:, :MEM_WIDTH].T.astype(BF16),
                    w_mem_kv[:, MEM_WIDTH:].astype(BF16))
    ya, yc = _local(x, row(pre_norm_w), w_loc, pool_mix_w.astype(BF16), row(pool_scale), kt, v)
    yb = _delta(x, row(pre_norm_w), w_qkv, w_ab, w_zd, conv_w.astype(F32), pad_lanes(a_log),
                pad_lanes(dt_bias), row(dn_norm_w))
    out = _merge(x.reshape(b * s, d), row(pre_norm_w), w_gate, ya.reshape(b * s, -1),
                 yb.reshape(b * s, -1), yc.reshape(b * s, -1), w_proj_pool.astype(BF16),
                 w_proj_delta.astype(BF16), w_proj_mem.astype(BF16), w_out.astype(BF16),
                 row(post_norm_w))
    return out.reshape(b, s, d)


def kernel(x, mem, pre_norm_w, mem_norm_w, w_in, conv_w, a_log, dt_bias, dn_norm_w, pool_mix_w,
           pool_scale, w_mem_kv, w_proj_pool, w_proj_delta, w_proj_mem, w_out, post_norm_w):
    for l in range(pre_norm_w.shape[0]):
        x = _layer(x, mem, pre_norm_w[l], mem_norm_w[l], w_in[l], conv_w[l], a_log[l], dt_bias[l],
                   dn_norm_w[l], pool_mix_w[l], pool_scale[l], w_mem_kv[l], w_proj_pool[l],
                   w_proj_delta[l], w_proj_mem[l], w_out[l], post_norm_w[l])
    return x
```

```python
import jax
import jax.numpy as jnp
from jax import lax
from jax.experimental import pallas as pl
from jax.experimental.pallas import tpu as pltpu

F32 = jnp.float32
BF16 = jnp.bfloat16

D_MODEL = 1024
HEAD_DIM = 128
DN_HEADS = D_MODEL // HEAD_DIM
DN_WIDTH = DN_HEADS * HEAD_DIM
POOL_WINDOWS = (2, 4, 8, 16)
POOL_GROUPS = len(POOL_WINDOWS)
POOL_WIDTH = D_MODEL // 2
POOL_GROUP_DIM = POOL_WIDTH // POOL_GROUPS
MEM_HEADS = 4
MEM_WIDTH = D_MODEL // 2
MEM_HEAD_DIM = MEM_WIDTH // MEM_HEADS
CONV_WIDTH = 4
CHUNK = 64
N_BRANCH = 3
EPS = 1e-6

OFF_XA = 0
OFF_ZA = OFF_XA + POOL_WIDTH
OFF_Q = OFF_ZA + POOL_WIDTH
OFF_K = OFF_Q + DN_WIDTH
OFF_V = OFF_K + DN_WIDTH
OFF_A = OFF_V + DN_WIDTH
OFF_B = OFF_A + DN_HEADS
OFF_ZD = OFF_B + DN_HEADS
OFF_QM = OFF_ZD + DN_WIDTH
OFF_ZM = OFF_QM + MEM_WIDTH
OFF_GATE = OFF_ZM + MEM_WIDTH

LANES = 128
SUBLANES = 8
SPAN = 2 * CHUNK
POOL_HALO = 16
CONV_HALO = SUBLANES
VMEM_LIMIT_BYTES = 56 * 1024 * 1024

TILE_LOCAL = 512
TILE_DELTA = 512
TILE_MERGE = 512


def _rms(x, w):
    return x * lax.rsqrt(jnp.mean(x * x, axis=-1, keepdims=True) + EPS) * w


def _sigmoid(x):
    return 1.0 / (1.0 + jnp.exp(-x))


def _silu(x):
    return x * _sigmoid(x)


def _dot(a, b):
    return jnp.dot(a, b, preferred_element_type=F32)


def _dot_nt(a, b):
    return lax.dot_general(a, b, (((1,), (1,)), ((), ())), preferred_element_type=F32)


def _mem_kv_kernel(mem_ref, nw_ref, wkt_ref, wv_ref, kt_ref, v_ref):
    mn = _rms(mem_ref[...], nw_ref[...]).astype(BF16)
    kt_ref[...] = _dot_nt(wkt_ref[...], mn).astype(BF16)
    v_ref[...] = _dot(mn, wv_ref[...]).astype(BF16)


def _mem_kv(mem, mem_norm_w, w_kt, w_v):
    b, m, d = mem.shape
    return pl.pallas_call(
        _mem_kv_kernel,
        grid=(b,),
        in_specs=[
            pl.BlockSpec((None, m, d), lambda i: (i, 0, 0)),
            pl.BlockSpec((1, d), lambda i: (0, 0)),
            pl.BlockSpec((MEM_WIDTH, d), lambda i: (0, 0)),
            pl.BlockSpec((d, MEM_WIDTH), lambda i: (0, 0)),
        ],
        out_specs=[
            pl.BlockSpec((None, MEM_WIDTH, m), lambda i: (i, 0, 0)),
            pl.BlockSpec((None, m, MEM_WIDTH), lambda i: (i, 0, 0)),
        ],
        out_shape=[
            jax.ShapeDtypeStruct((b, MEM_WIDTH, m), BF16),
            jax.ShapeDtypeStruct((b, m, MEM_WIDTH), BF16),
        ],
        compiler_params=pltpu.CompilerParams(
            dimension_semantics=("arbitrary",), vmem_limit_bytes=VMEM_LIMIT_BYTES),
        name="mem_kv",
    )(mem, mem_norm_w, w_kt, w_v)


def _local_kernel(x_ref, nw_ref, w_ref, mix_ref, scale_ref, kt_ref, v_ref, ya_ref, yc_ref,
                  halo_ref):
    tm = x_ref.shape[0]
    s = pl.program_id(1)

    @pl.when(s == 0)
    def _():
        halo_ref[...] = jnp.zeros_like(halo_ref)

    h = _rms(x_ref[...], nw_ref[...]).astype(BF16)
    proj = _dot(h, w_ref[...])
    xa = proj[:, 0:POOL_WIDTH]
    za = proj[:, POOL_WIDTH:2 * POOL_WIDTH]
    qm = proj[:, 2 * POOL_WIDTH:2 * POOL_WIDTH + MEM_WIDTH]
    zm = proj[:, 2 * POOL_WIDTH + MEM_WIDTH:]

    ext = jnp.concatenate([halo_ref[...], xa], axis=0)
    halo_ref[...] = xa[tm - POOL_HALO:, :]
    pos = (s * tm + 1 + lax.broadcasted_iota(jnp.int32, (tm, 1), 0)).astype(F32)
    mixed = []
    for g, window in enumerate(POOL_WINDOWS):
        cols = slice(g * POOL_GROUP_DIM, (g + 1) * POOL_GROUP_DIM)
        acc = ext[:, cols]
        width = 1
        while width < window:
            acc = acc + pltpu.roll(acc, width, 0)
            width *= 2
        mean = acc[POOL_HALO:, :] / jnp.minimum(pos, float(window))
        p = mean - xa[:, cols]
        mixed.append(_dot(p.astype(BF16), mix_ref[g]))
    m = jnp.concatenate(mixed, axis=1) * scale_ref[...]
    ya_ref[...] = (m * _silu(za)).astype(BF16)

    heads = []
    for hh in range(MEM_HEADS):
        cols = slice(hh * MEM_HEAD_DIM, (hh + 1) * MEM_HEAD_DIM)
        sc = _dot(qm[:, cols].astype(BF16), kt_ref[cols, :]) * (MEM_HEAD_DIM ** -0.5)
        e = jnp.exp(sc - jnp.max(sc, axis=-1, keepdims=True))
        o = _dot(e.astype(BF16), v_ref[:, cols])
        heads.append(o / jnp.sum(e, axis=-1, keepdims=True))
    yc_ref[...] = (jnp.concatenate(heads, axis=1) * _silu(zm)).astype(BF16)


def _local(x, pre_norm_w, w_loc, mix_w, pool_scale, kt, v):
    b, s, d = x.shape
    tm = TILE_LOCAL
    m = kt.shape[-1]
    const2 = lambda i, j: (0, 0)
    return pl.pallas_call(
        _local_kernel,
        grid=(b, s // tm),
        in_specs=[
            pl.BlockSpec((None, tm, d), lambda i, j: (i, j, 0)),
            pl.BlockSpec((1, d), const2),
            pl.BlockSpec(w_loc.shape, const2),
            pl.BlockSpec(mix_w.shape, lambda i, j: (0, 0, 0)),
            pl.BlockSpec((1, POOL_WIDTH), const2),
            pl.BlockSpec((None, MEM_WIDTH, m), lambda i, j: (i, 0, 0)),
            pl.BlockSpec((None, m, MEM_WIDTH), lambda i, j: (i, 0, 0)),
        ],
        out_specs=[
            pl.BlockSpec((None, tm, POOL_WIDTH), lambda i, j: (i, j, 0)),
            pl.BlockSpec((None, tm, MEM_WIDTH), lambda i, j: (i, j, 0)),
        ],
        out_shape=[
            jax.ShapeDtypeStruct((b, s, POOL_WIDTH), BF16),
            jax.ShapeDtypeStruct((b, s, MEM_WIDTH), BF16),
        ],
        scratch_shapes=[pltpu.VMEM((POOL_HALO, POOL_WIDTH), F32)],
        compiler_params=pltpu.CompilerParams(
            dimension_semantics=("arbitrary", "arbitrary"), vmem_limit_bytes=VMEM_LIMIT_BYTES),
        name="local_branches",
    )(x, pre_norm_w, w_loc, mix_w, pool_scale, kt, v)


def _delta_kernel(x_ref, nw_ref, wqkv_ref, wab_ref, wzd_ref, convw_ref, alog_ref, dtb_ref,
                  dnw_ref, yb_ref, qkv_buf, state_ref, zs_buf, gc_buf, beta_buf, gct_buf,
                  kq_lhs, kn_b, rhs_b, wq_b, kdt_b, aqk_b, tp_b, vn_b, t_f, u_f, oi_f):
    tm = x_ref.shape[0]
    n_span = tm // SPAN
    s = pl.program_id(1)

    @pl.when(s == 0)
    def _():
        qkv_buf[0:CONV_HALO, :] = jnp.zeros((CONV_HALO, 3 * DN_WIDTH), F32)
        state_ref[...] = jnp.zeros_like(state_ref)

    @pl.when(s > 0)
    def _():
        qkv_buf[0:CONV_HALO, :] = qkv_buf[tm:tm + CONV_HALO, :]

    h = _rms(x_ref[...], nw_ref[...]).astype(BF16)
    qkv_buf[CONV_HALO:CONV_HALO + tm, :] = _dot(h, wqkv_ref[...])
    zs_buf[...] = _silu(_dot(h, wzd_ref[...]))

    ab = _dot(h, wab_ref[...])
    sp_in = ab + dtb_ref[...]
    softplus = jnp.maximum(sp_in, 0.0) + jnp.log1p(jnp.exp(-jnp.abs(sp_in)))
    g = -jnp.exp(alog_ref[...]) * softplus
    beta_buf[...] = _sigmoid(ab)

    row_in_chunk = lax.broadcasted_iota(jnp.int32, (tm, LANES), 0) % CHUNK
    gc = g
    shift = 1
    while shift < CHUNK:
        gc = gc + jnp.where(row_in_chunk >= shift, pltpu.roll(gc, shift, 0), 0.0)
        shift *= 2
    gc_buf[...] = gc
    for i in range(n_span):
        gct_buf[i] = gc[i * SPAN:(i + 1) * SPAN, :].T

    ri = lax.broadcasted_iota(jnp.int32, (SPAN, SPAN), 0)
    ci = lax.broadcasted_iota(jnp.int32, (SPAN, SPAN), 1)
    same_chunk = (ri // CHUNK) == (ci // CHUNK)
    tril = same_chunk & (ri >= ci)
    stril = same_chunk & (ri > ci)
    eye = (ri == ci).astype(F32)
    first_chunk_rows = lax.broadcasted_iota(jnp.int32, (SPAN, LANES), 0) < CHUNK

    def conv_silu(r0, c0):
        blk = qkv_buf[pl.ds(r0, SPAN + CONV_HALO), c0:c0 + HEAD_DIM]
        w = convw_ref[:, c0:c0 + HEAD_DIM]
        acc = blk * w[CONV_WIDTH - 1:CONV_WIDTH, :]
        for back in range(1, CONV_WIDTH):
            tap = CONV_WIDTH - 1 - back
            acc = acc + pltpu.roll(blk, back, 0) * w[tap:tap + 1, :]
        return _silu(acc[CONV_HALO:, :])

    def span_body(sp, carry):
        r0 = pl.multiple_of(sp * SPAN, SPAN)
        gc_s = gc_buf[pl.ds(r0, SPAN), :]
        beta_s = beta_buf[pl.ds(r0, SPAN), :]
        gct_s = gct_buf[sp]
        last0 = gc_s[CHUNK - 1:CHUNK, :]
        last1 = gc_s[SPAN - 1:SPAN, :]
        glast = jnp.where(first_chunk_rows, last0, last1)
        e_s = jnp.exp(gc_s)
        kdf_s = jnp.exp(glast - gc_s)
        eg0 = jnp.exp(last0)
        eg1 = jnp.exp(last1)

        heads = range(DN_HEADS)
        top = slice(0, SPAN)
        bot = slice(SPAN, 2 * SPAN)

        for hd in heads:
            q = conv_silu(r0, hd * HEAD_DIM)
            k = conv_silu(r0, DN_WIDTH + hd * HEAD_DIM)
            v = conv_silu(r0, 2 * DN_WIDTH + hd * HEAD_DIM)
            qn = q * lax.rsqrt(jnp.sum(q * q, axis=-1, keepdims=True) + EPS) * (HEAD_DIM ** -0.5)
            kn = k * lax.rsqrt(jnp.sum(k * k, axis=-1, keepdims=True) + EPS)
            bcol = beta_s[:, DN_HEADS + hd:DN_HEADS + hd + 1]
            ecol = e_s[:, hd:hd + 1]
            kb = kn * bcol
            kq_lhs[hd, top, :] = kb.astype(BF16)
            kq_lhs[hd, bot, :] = qn.astype(BF16)
            kn_b[hd] = kn.astype(BF16)
            rhs_b[hd, :, 0:HEAD_DIM] = (v * bcol).astype(BF16)
            rhs_b[hd, :, HEAD_DIM:] = (kb * ecol).astype(BF16)
            qg = (qn * ecol).astype(BF16)
            for c in range(2):
                wq_b[hd, c, CHUNK:, :] = qg[c * CHUNK:(c + 1) * CHUNK, :]
            kdt_b[hd] = (kn * kdf_s[:, hd:hd + 1]).T.astype(BF16)

        for hd in heads:
            diff = gc_s[:, hd:hd + 1] - gct_s[hd:hd + 1, :]
            decay = jnp.where(tril, jnp.exp(jnp.where(tril, diff, 0.0)), 0.0)
            kq = _dot_nt(kq_lhs[hd], kn_b[hd])
            a = jnp.where(stril, kq[top, :] * decay, 0.0)
            aqk_b[hd] = (kq[bot, :] * decay).astype(BF16)
            t0 = eye - a
            t_f[hd] = t0
            tp_b[hd, top, :] = t0.astype(BF16)
            tp_b[hd, bot, :] = a.astype(BF16)

        for hd in heads:
            a_b = tp_b[hd, bot, :]
            tp_b[hd, bot, :] = _dot(a_b, a_b).astype(BF16)
        order = 2
        while 2 * order < CHUNK:
            for hd in heads:
                r = _dot(tp_b[hd], tp_b[hd, bot, :])
                t_new = t_f[hd] + r[top, :]
                t_f[hd] = t_new
                tp_b[hd, top, :] = t_new.astype(BF16)
                tp_b[hd, bot, :] = r[bot, :].astype(BF16)
            order *= 2
        for hd in heads:
            t_new = t_f[hd] + _dot(tp_b[hd, top, :], tp_b[hd, bot, :])
            tp_b[hd, top, :] = t_new.astype(BF16)

        for hd in heads:
            uw = _dot(tp_b[hd, top, :], rhs_b[hd])
            u_f[hd] = uw[:, :HEAD_DIM]
            w = uw[:, HEAD_DIM:].astype(BF16)
            for c in range(2):
                wq_b[hd, c, 0:CHUNK, :] = w[c * CHUNK:(c + 1) * CHUNK, :]

        zeros_chunk = jnp.zeros((CHUNK, HEAD_DIM), BF16)
        for c, eg in ((0, eg0), (1, eg1)):
            rows = slice(c * CHUNK, (c + 1) * CHUNK)
            for hd in heads:
                r = _dot(wq_b[hd, c], state_ref[hd].astype(BF16))
                vn_b[hd, rows, :] = (u_f[hd, rows, :] - r[:CHUNK, :]).astype(BF16)
                oi_f[hd, rows, :] = r[CHUNK:, :]
            for hd in heads:
                padded = [zeros_chunk, zeros_chunk]
                padded[c] = vn_b[hd, rows, :]
                upd = _dot(kdt_b[hd], jnp.concatenate(padded, axis=0))
                state_ref[hd] = state_ref[hd] * eg[:, hd:hd + 1] + upd

        for hd in heads:
            o = oi_f[hd] + _dot(aqk_b[hd], vn_b[hd])
            cols = slice(hd * HEAD_DIM, (hd + 1) * HEAD_DIM)
            y = _rms(o, dnw_ref[...]) * zs_buf[pl.ds(r0, SPAN), cols]
            yb_ref[pl.ds(r0, SPAN), cols] = y.astype(BF16)
        return carry

    lax.fori_loop(0, n_span, span_body, 0)


def _delta(x, pre_norm_w, w_qkv, w_ab, w_zd, conv_w, a_log, dt_bias, dn_norm_w):
    b, s, d = x.shape
    tm = TILE_DELTA
    const2 = lambda i, j: (0, 0)
    return pl.pallas_call(
        _delta_kernel,
        grid=(b, s // tm),
        in_specs=[
            pl.BlockSpec((None, tm, d), lambda i, j: (i, j, 0)),
            pl.BlockSpec((1, d), const2),
            pl.BlockSpec(w_qkv.shape, const2),
            pl.BlockSpec(w_ab.shape, const2),
            pl.BlockSpec(w_zd.shape, const2),
            pl.BlockSpec(conv_w.shape, const2),
            pl.BlockSpec((1, LANES), const2),
            pl.BlockSpec((1, LANES), const2),
            pl.BlockSpec((1, HEAD_DIM), const2),
        ],
        out_specs=pl.BlockSpec((None, tm, DN_WIDTH), lambda i, j: (i, j, 0)),
        out_shape=jax.ShapeDtypeStruct((b, s, DN_WIDTH), BF16),
        scratch_shapes=[
            pltpu.VMEM((CONV_HALO + tm, 3 * DN_WIDTH), F32),
            pltpu.VMEM((DN_HEADS, HEAD_DIM, HEAD_DIM), F32),
            pltpu.VMEM((tm, DN_WIDTH), F32),
            pltpu.VMEM((tm, LANES), F32),
            pltpu.VMEM((tm, LANES), F32),
            pltpu.VMEM((tm // SPAN, LANES, SPAN), F32),
            pltpu.VMEM((DN_HEADS, 2 * SPAN, HEAD_DIM), BF16),
            pltpu.VMEM((DN_HEADS, SPAN, HEAD_DIM), BF16),
            pltpu.VMEM((DN_HEADS, SPAN, 2 * HEAD_DIM), BF16),
            pltpu.VMEM((DN_HEADS, 2, SPAN, HEAD_DIM), BF16),
            pltpu.VMEM((DN_HEADS, HEAD_DIM, SPAN), BF16),
            pltpu.VMEM((DN_HEADS, SPAN, SPAN), BF16),
            pltpu.VMEM((DN_HEADS, 2 * SPAN, SPAN), BF16),
            pltpu.VMEM((DN_HEADS, SPAN, HEAD_DIM), BF16),
            pltpu.VMEM((DN_HEADS, SPAN, SPAN), F32),
            pltpu.VMEM((DN_HEADS, SPAN, HEAD_DIM), F32),
            pltpu.VMEM((DN_HEADS, SPAN, HEAD_DIM), F32),
        ],
        compiler_params=pltpu.CompilerParams(
            dimension_semantics=("arbitrary", "arbitrary"), vmem_limit_bytes=VMEM_LIMIT_BYTES),
        name="delta_branch",
    )(x, pre_norm_w, w_qkv, w_ab, w_zd, conv_w, a_log, dt_bias, dn_norm_w)


def _merge_kernel(x_ref, nw_ref, wg_ref, ya_ref, yb_ref, yc_ref, wpa_ref, wpb_ref, wpc_ref,
                  wout_ref, pw_ref, o_ref):
    x = x_ref[...]
    h = _rms(x, nw_ref[...]).astype(BF16)
    y = None
    for br, (y_ref, w_ref) in enumerate(((ya_ref, wpa_ref), (yb_ref, wpb_ref), (yc_ref, wpc_ref))):
        gate = _sigmoid(_dot(h, wg_ref[:, br * D_MODEL:(br + 1) * D_MODEL]))
        term = gate * _dot(y_ref[...], w_ref[...])
        y = term if y is None else y + term
    out = _dot(y.astype(BF16), wout_ref[...])
    o_ref[...] = x + _rms(out, pw_ref[...])


def _merge(x2, pre_norm_w, w_gate, ya, yb, yc, w_pa, w_pb, w_pc, w_out, post_norm_w):
    n, d = x2.shape
    tm = TILE_MERGE
    const = lambda i: (0, 0)
    row = lambda i: (i, 0)
    return pl.pallas_call(
        _merge_kernel,
        grid=(n // tm,),
        in_specs=[
            pl.BlockSpec((tm, d), row),
            pl.BlockSpec((1, d), const),
            pl.BlockSpec(w_gate.shape, const),
            pl.BlockSpec((tm, POOL_WIDTH), row),
            pl.BlockSpec((tm, DN_WIDTH), row),
            pl.BlockSpec((tm, MEM_WIDTH), row),
            pl.BlockSpec(w_pa.shape, const),
            pl.BlockSpec(w_pb.shape, const),
            pl.BlockSpec(w_pc.shape, const),
            pl.BlockSpec(w_out.shape, const),
            pl.BlockSpec((1, d), const),
        ],
        out_specs=pl.BlockSpec((tm, d), row),
        out_shape=jax.ShapeDtypeStruct((n, d), F32),
        compiler_params=pltpu.CompilerParams(
            dimension_semantics=("arbitrary",), vmem_limit_bytes=VMEM_LIMIT_BYTES),
        name="merge_out",
    )(x2, pre_norm_w, w_gate, ya, yb, yc, w_pa, w_pb, w_pc, w_out, post_norm_w)


def _layer(x, mem, pre_norm_w, mem_norm_w, w_in, conv_w, a_log, dt_bias, dn_norm_w, pool_mix_w,
           pool_scale, w_mem_kv, w_proj_pool, w_proj_delta, w_proj_mem, w_out, post_norm_w):
    b, s, d = x.shape
    assert d == D_MODEL and s % TILE_LOCAL == 0 and s % TILE_DELTA == 0
    assert (b * s) % TILE_MERGE == 0 and TILE_DELTA % SPAN == 0
    row = lambda v: v.reshape(1, -1).astype(F32)
    pad_lanes = lambda v: jnp.pad(v.astype(F32), (0, LANES - v.shape[0])).reshape(1, LANES)

    w_in_b = w_in.astype(BF16)
    w_loc = jnp.concatenate(
        [w_in_b[:, OFF_XA:OFF_Q], w_in_b[:, OFF_QM:OFF_GATE]], axis=1)
    w_qkv = w_in_b[:, OFF_Q:OFF_A]
    w_ab = jnp.pad(w_in_b[:, OFF_A:OFF_ZD], ((0, 0), (0, LANES - 2 * DN_HEADS)))
    w_zd = w_in_b[:, OFF_ZD:OFF_QM]
    w_gate = w_in_b[:, OFF_GATE:]

    kt, v = _mem_kv(mem, row(mem_norm_w), w_mem_kv[:, :MEM_WIDTH].T.astype(BF16),
                    w_mem_kv[:, MEM_WIDTH:].astype(BF16))
    ya, yc = _local(x, row(pre_norm_w), w_loc, pool_mix_w.astype(BF16), row(pool_scale), kt, v)
    yb = _delta(x, row(pre_norm_w), w_qkv, w_ab, w_zd, conv_w.astype(F32), pad_lanes(a_log),
                pad_lanes(dt_bias), row(dn_norm_w))
    out = _merge(x.reshape(b * s, d), row(pre_norm_w), w_gate, ya.reshape(b * s, -1),
                 yb.reshape(b * s, -1), yc.reshape(b * s, -1), w_proj_pool.astype(BF16),
                 w_proj_delta.astype(BF16), w_proj_mem.astype(BF16), w_out.astype(BF16),
                 row(post_norm_w))
    return out.reshape(b, s, d)


def kernel(x, mem, pre_norm_w, mem_norm_w, w_in, conv_w, a_log, dt_bias, dn_norm_w, pool_mix_w,
           pool_scale, w_mem_kv, w_proj_pool, w_proj_delta, w_proj_mem, w_out, post_norm_w):
    for l in range(pre_norm_w.shape[0]):
        x = _layer(x, mem, pre_norm_w[l], mem_norm_w[l], w_in[l], conv_w[l], a_log[l], dt_bias[l],
                   dn_norm_w[l], pool_mix_w[l], pool_scale[l], w_mem_kv[l], w_proj_pool[l],
                   w_proj_delta[l], w_proj_mem[l], w_out[l], post_norm_w[l])
    return x
```

```python
import jax
import jax.numpy as jnp
from jax import lax
from jax.experimental import pallas as pl
from jax.experimental.pallas import tpu as pltpu

F32 = jnp.float32
BF16 = jnp.bfloat16

D_MODEL = 1024
HEAD_DIM = 128
DN_HEADS = D_MODEL // HEAD_DIM
DN_WIDTH = DN_HEADS * HEAD_DIM
POOL_WINDOWS = (2, 4, 8, 16)
POOL_GROUPS = len(POOL_WINDOWS)
POOL_WIDTH = D_MODEL // 2
POOL_GROUP_DIM = POOL_WIDTH // POOL_GROUPS
MEM_HEADS = 4
MEM_WIDTH = D_MODEL // 2
MEM_HEAD_DIM = MEM_WIDTH // MEM_HEADS
CONV_WIDTH = 4
CHUNK = 64
N_BRANCH = 3
EPS = 1e-6

OFF_XA = 0
OFF_ZA = OFF_XA + POOL_WIDTH
OFF_Q = OFF_ZA + POOL_WIDTH
OFF_K = OFF_Q + DN_WIDTH
OFF_V = OFF_K + DN_WIDTH
OFF_A = OFF_V + DN_WIDTH
OFF_B = OFF_A + DN_HEADS
OFF_ZD = OFF_B + DN_HEADS
OFF_QM = OFF_ZD + DN_WIDTH
OFF_ZM = OFF_QM + MEM_WIDTH
OFF_GATE = OFF_ZM + MEM_WIDTH

LANES = 128
SUBLANES = 8
SPAN = 2 * CHUNK
POOL_HALO = 16
CONV_HALO = SUBLANES
HEAD_GROUP = 2
GROUP_W = HEAD_GROUP * HEAD_DIM
RAW_SLOTS = 4
VMEM_LIMIT_BYTES = 56 * 1024 * 1024

TILE_LOCAL = 512
TILE_DELTA = 512
TILE_MERGE = 512


def _rms(x, w):
    return x * lax.rsqrt(jnp.mean(x * x, axis=-1, keepdims=True) + EPS) * w


def _sigmoid(x):
    return 1.0 / (1.0 + jnp.exp(-x))


def _silu(x):
    return x * _sigmoid(x)


def _dot(a, b):
    return jnp.dot(a, b, preferred_element_type=F32)


def _dot_nt(a, b):
    return lax.dot_general(a, b, (((1,), (1,)), ((), ())), preferred_element_type=F32)


def _mem_kv_kernel(mem_ref, nw_ref, wkt_ref, wv_ref, kt_ref, v_ref):
    mn = _rms(mem_ref[...], nw_ref[...]).astype(BF16)
    kt_ref[...] = _dot_nt(wkt_ref[...], mn).astype(BF16)
    v_ref[...] = _dot(mn, wv_ref[...]).astype(BF16)


def _mem_kv(mem, mem_norm_w, w_kt, w_v):
    b, m, d = mem.shape
    return pl.pallas_call(
        _mem_kv_kernel,
        grid=(b,),
        in_specs=[
            pl.BlockSpec((None, m, d), lambda i: (i, 0, 0)),
            pl.BlockSpec((1, d), lambda i: (0, 0)),
            pl.BlockSpec((MEM_WIDTH, d), lambda i: (0, 0)),
            pl.BlockSpec((d, MEM_WIDTH), lambda i: (0, 0)),
        ],
        out_specs=[
            pl.BlockSpec((None, MEM_WIDTH, m), lambda i: (i, 0, 0)),
            pl.BlockSpec((None, m, MEM_WIDTH), lambda i: (i, 0, 0)),
        ],
        out_shape=[
            jax.ShapeDtypeStruct((b, MEM_WIDTH, m), BF16),
            jax.ShapeDtypeStruct((b, m, MEM_WIDTH), BF16),
        ],
        compiler_params=pltpu.CompilerParams(
            dimension_semantics=("arbitrary",), vmem_limit_bytes=VMEM_LIMIT_BYTES),
        name="mem_kv",
    )(mem, mem_norm_w, w_kt, w_v)


def _local_kernel(x_ref, nw_ref, w_ref, mix_ref, scale_ref, kt_ref, v_ref, ya_ref, yc_ref,
                  halo_ref):
    tm = x_ref.shape[0]
    s = pl.program_id(1)

    @pl.when(s == 0)
    def _():
        halo_ref[...] = jnp.zeros_like(halo_ref)

    h = _rms(x_ref[...], nw_ref[...]).astype(BF16)
    proj = _dot(h, w_ref[...])
    xa = proj[:, 0:POOL_WIDTH]
    za = proj[:, POOL_WIDTH:2 * POOL_WIDTH]
    qm = proj[:, 2 * POOL_WIDTH:2 * POOL_WIDTH + MEM_WIDTH]
    zm = proj[:, 2 * POOL_WIDTH + MEM_WIDTH:]

    ext = jnp.concatenate([halo_ref[...], xa], axis=0)
    halo_ref[...] = xa[tm - POOL_HALO:, :]
    pos = (s * tm + 1 + lax.broadcasted_iota(jnp.int32, (tm, 1), 0)).astype(F32)
    mixed = []
    for g, window in enumerate(POOL_WINDOWS):
        cols = slice(g * POOL_GROUP_DIM, (g + 1) * POOL_GROUP_DIM)
        acc = ext[:, cols]
        width = 1
        while width < window:
            acc = acc + pltpu.roll(acc, width, 0)
            width *= 2
        mean = acc[POOL_HALO:, :] / jnp.minimum(pos, float(window))
        p = mean - xa[:, cols]
        mixed.append(_dot(p.astype(BF16), mix_ref[g]))
    m = jnp.concatenate(mixed, axis=1) * scale_ref[...]
    ya_ref[...] = (m * _silu(za)).astype(BF16)

    heads = []
    for hh in range(MEM_HEADS):
        cols = slice(hh * MEM_HEAD_DIM, (hh + 1) * MEM_HEAD_DIM)
        sc = _dot(qm[:, cols].astype(BF16), kt_ref[cols, :]) * (MEM_HEAD_DIM ** -0.5)
        e = jnp.exp(sc - jnp.max(sc, axis=-1, keepdims=True))
        o = _dot(e.astype(BF16), v_ref[:, cols])
        heads.append(o / jnp.sum(e, axis=-1, keepdims=True))
    yc_ref[...] = (jnp.concatenate(heads, axis=1) * _silu(zm)).astype(BF16)


def _local(x, pre_norm_w, w_loc, mix_w, pool_scale, kt, v):
    b, s, d = x.shape
    tm = TILE_LOCAL
    m = kt.shape[-1]
    const2 = lambda i, j: (0, 0)
    return pl.pallas_call(
        _local_kernel,
        grid=(b, s // tm),
        in_specs=[
            pl.BlockSpec((None, tm, d), lambda i, j: (i, j, 0)),
            pl.BlockSpec((1, d), const2),
            pl.BlockSpec(w_loc.shape, const2),
            pl.BlockSpec(mix_w.shape, lambda i, j: (0, 0, 0)),
            pl.BlockSpec((1, POOL_WIDTH), const2),
            pl.BlockSpec((None, MEM_WIDTH, m), lambda i, j: (i, 0, 0)),
            pl.BlockSpec((None, m, MEM_WIDTH), lambda i, j: (i, 0, 0)),
        ],
        out_specs=[
            pl.BlockSpec((None, tm, POOL_WIDTH), lambda i, j: (i, j, 0)),
            pl.BlockSpec((None, tm, MEM_WIDTH), lambda i, j: (i, j, 0)),
        ],
        out_shape=[
            jax.ShapeDtypeStruct((b, s, POOL_WIDTH), BF16),
            jax.ShapeDtypeStruct((b, s, MEM_WIDTH), BF16),
        ],
        scratch_shapes=[pltpu.VMEM((POOL_HALO, POOL_WIDTH), F32)],
        compiler_params=pltpu.CompilerParams(
            dimension_semantics=("arbitrary", "arbitrary"), vmem_limit_bytes=VMEM_LIMIT_BYTES),
        name="local_branches",
    )(x, pre_norm_w, w_loc, mix_w, pool_scale, kt, v)


def _delta_kernel(x_ref, nw_ref, wqkv_ref, wab_ref, wzd_ref, convw_ref, alog_ref, dtb_ref,
                  dnw_ref, yb_ref, halo_ref, state_ref, zs_buf, e_buf, raw_buf,
                  kq_lhs, kn_b, rhs_b, wq_b, kdt_b, aqk_b, u_f, tp_b, t_f, vn_b, oi_f):
    tm = x_ref.shape[0]
    n_span = tm // SPAN
    n_chunk = tm // CHUNK
    s = pl.program_id(1)
    heads = range(DN_HEADS)
    top = slice(0, SPAN)
    bot = slice(SPAN, 2 * SPAN)

    @pl.when(s == 0)
    def _():
        halo_ref[...] = jnp.zeros_like(halo_ref)
        state_ref[...] = jnp.zeros_like(state_ref)

    h = _rms(x_ref[...], nw_ref[...]).astype(BF16)

    ab = _dot(h, wab_ref[...])
    sp_in = ab + dtb_ref[...]
    softplus = jnp.maximum(sp_in, 0.0) + jnp.log1p(jnp.exp(-jnp.abs(sp_in)))
    g = -jnp.exp(alog_ref[...]) * softplus
    beta = _sigmoid(ab)

    row_in_chunk = lax.broadcasted_iota(jnp.int32, (tm, LANES), 0) % CHUNK
    gc = g
    shift = 1
    while shift < CHUNK:
        gc = gc + jnp.where(row_in_chunk >= shift, pltpu.roll(gc, shift, 0), 0.0)
        shift *= 2
    e = jnp.exp(gc)
    e_buf[...] = e
    glast = jnp.concatenate(
        [jnp.broadcast_to(gc[(c + 1) * CHUNK - 1:(c + 1) * CHUNK, :], (CHUNK, LANES))
         for c in range(n_chunk)], axis=0)
    kdf = jnp.exp(glast - gc)

    ri = lax.broadcasted_iota(jnp.int32, (SPAN, SPAN), 0)
    ci = lax.broadcasted_iota(jnp.int32, (SPAN, SPAN), 1)
    same_chunk = (ri // CHUNK) == (ci // CHUNK)
    tril = same_chunk & (ri >= ci)
    stril = same_chunk & (ri > ci)
    eye = (ri == ci).astype(F32)
    gct = [gc[i * SPAN:(i + 1) * SPAN, :].T for i in range(n_span)]

    n_raw = raw_buf.shape[0]

    def conv_silu(which, gi):
        blk = which * (DN_HEADS // HEAD_GROUP) + gi
        slot = blk % n_raw
        bcols = slice(blk * GROUP_W, (blk + 1) * GROUP_W)
        raw = _dot(h, wqkv_ref[blk])
        raw_buf[slot, 0:CONV_HALO, :] = halo_ref[:, bcols]
        raw_buf[slot, CONV_HALO:, :] = raw
        halo_ref[:, bcols] = raw[tm - CONV_HALO:, :]
        w = convw_ref[blk]
        acc = raw * w[CONV_WIDTH - 1:CONV_WIDTH, :]
        for back in range(1, CONV_WIDTH):
            tap = CONV_WIDTH - 1 - back
            start = CONV_HALO - back
            acc = acc + raw_buf[slot, start:start + tm, :] * w[tap:tap + 1, :]
        out = _silu(acc)
        return [out[:, j * HEAD_DIM:(j + 1) * HEAD_DIM] for j in range(HEAD_GROUP)]

    def prep_k(gi):
        for j, k in enumerate(conv_silu(1, gi)):
            hd = gi * HEAD_GROUP + j
            kn = k * lax.rsqrt(jnp.sum(k * k, axis=-1, keepdims=True) + EPS)
            kb = kn * beta[:, DN_HEADS + hd:DN_HEADS + hd + 1]
            kb_b = kb.astype(BF16)
            kd = kn * kdf[:, hd:hd + 1]
            kn_b[hd] = kn.astype(BF16)
            rhs_b[hd, :, HEAD_DIM:] = (kb * e[:, hd:hd + 1]).astype(BF16)
            for i in range(n_span):
                rows = slice(i * SPAN, (i + 1) * SPAN)
                kq_lhs[hd, i, top, :] = kb_b[rows, :]
                kdt_b[hd, i] = kd[rows, :].T.astype(BF16)

    def prep_q(gi):
        for j, q in enumerate(conv_silu(0, gi)):
            hd = gi * HEAD_GROUP + j
            qn = q * lax.rsqrt(jnp.sum(q * q, axis=-1, keepdims=True) + EPS) * (HEAD_DIM ** -0.5)
            qn_b = qn.astype(BF16)
            qg = (qn * e[:, hd:hd + 1]).astype(BF16)
            for i in range(n_span):
                kq_lhs[hd, i, bot, :] = qn_b[i * SPAN:(i + 1) * SPAN, :]
            for c in range(n_chunk):
                wq_b[hd, c, CHUNK:, :] = qg[c * CHUNK:(c + 1) * CHUNK, :]

    def prep_v(gi):
        for j, v in enumerate(conv_silu(2, gi)):
            hd = gi * HEAD_GROUP + j
            rhs_b[hd, :, 0:HEAD_DIM] = (v * beta[:, DN_HEADS + hd:DN_HEADS + hd + 1]).astype(BF16)

    def gate_piece(blk):
        cols = slice(blk * GROUP_W, (blk + 1) * GROUP_W)
        zs_buf[:, cols] = _silu(_dot(h, wzd_ref[:, cols]))

    def prep_pieces(gi):
        return [(prep_k, gi), (prep_q, gi), (prep_v, gi)]

    def stage_score(items):
        for hd, i, sl in items:
            rows = slice(i * SPAN, (i + 1) * SPAN)
            diff = gc[rows, hd:hd + 1] - gct[i][hd:hd + 1, :]
            decay = jnp.where(tril, jnp.exp(jnp.where(tril, diff, 0.0)), 0.0)
            kq = _dot_nt(kq_lhs[hd, i], kn_b[hd, rows, :])
            a = jnp.where(stril, kq[top, :] * decay, 0.0)
            aqk_b[hd, i] = (kq[bot, :] * decay).astype(BF16)
            t0 = eye - a
            t_f[sl, i] = t0
            tp_b[sl, i, top, :] = t0.astype(BF16)
            tp_b[sl, i, bot, :] = a.astype(BF16)

    def stage_square(items):
        for hd, i, sl in items:
            a_b = tp_b[sl, i, bot, :]
            tp_b[sl, i, bot, :] = _dot(a_b, a_b).astype(BF16)

    def stage_combined(items):
        for hd, i, sl in items:
            r = _dot(tp_b[sl, i], tp_b[sl, i, bot, :])
            t_new = t_f[sl, i] + r[top, :]
            t_f[sl, i] = t_new
            tp_b[sl, i, top, :] = t_new.astype(BF16)
            tp_b[sl, i, bot, :] = r[bot, :].astype(BF16)

    def stage_final(items):
        for hd, i, sl in items:
            t_new = t_f[sl, i] + _dot(tp_b[sl, i, top, :], tp_b[sl, i, bot, :])
            tp_b[sl, i, top, :] = t_new.astype(BF16)

    def stage_uw(items):
        for hd, i, sl in items:
            rows = slice(i * SPAN, (i + 1) * SPAN)
            uw = _dot(tp_b[sl, i, top, :], rhs_b[hd, rows, :])
            u_f[hd, rows, :] = uw[:, :HEAD_DIM]
            w = uw[:, HEAD_DIM:].astype(BF16)
            for c in range(2):
                wq_b[hd, 2 * i + c, 0:CHUNK, :] = w[c * CHUNK:(c + 1) * CHUNK, :]

    n_combined = 0
    order = 2
    while 2 * order < CHUNK:
        n_combined += 1
        order *= 2
    stages = [stage_score, stage_square] + [stage_combined] * n_combined + [stage_final, stage_uw]

    n_slot = tp_b.shape[0]
    groups = [range(g0, g0 + HEAD_GROUP) for g0 in range(0, DN_HEADS, HEAD_GROUP)]
    for fn, arg in prep_pieces(0):
        fn(arg)
    for gi, group in enumerate(groups):
        items = [(hd, i, hd % n_slot) for hd in group for i in range(n_span)]
        if gi + 1 < len(groups):
            fillers = prep_pieces(gi + 1)
        else:
            fillers = [(gate_piece, blk) for blk in range(DN_WIDTH // GROUP_W)]
        every = len(stages) // len(fillers)
        for n, stage in enumerate(stages):
            stage(items)
            if n % every == 0 and n // every < len(fillers):
                fn, arg = fillers[n // every]
                fn(arg)

    def span_body(sp, carry):
        r0 = pl.multiple_of(sp * SPAN, SPAN)
        e_s = e_buf[pl.ds(r0, SPAN), :]
        eg0 = e_s[CHUNK - 1:CHUNK, :]
        eg1 = e_s[SPAN - 1:SPAN, :]

        zeros_chunk = jnp.zeros((CHUNK, HEAD_DIM), BF16)
        for c, eg in ((0, eg0), (1, eg1)):
            rows = slice(c * CHUNK, (c + 1) * CHUNK)
            for hd in heads:
                r = _dot(wq_b[hd, 2 * sp + c], state_ref[hd].astype(BF16))
                u_c = u_f[hd, pl.ds(r0 + c * CHUNK, CHUNK), :]
                vn_b[hd, rows, :] = (u_c - r[:CHUNK, :]).astype(BF16)
                oi_f[hd, rows, :] = r[CHUNK:, :]
            for hd in heads:
                padded = [zeros_chunk, zeros_chunk]
                padded[c] = vn_b[hd, rows, :]
                upd = _dot(kdt_b[hd, sp], jnp.concatenate(padded, axis=0))
                state_ref[hd] = state_ref[hd] * eg[:, hd:hd + 1] + upd

        for hd in heads:
            o = oi_f[hd] + _dot(aqk_b[hd, sp], vn_b[hd])
            cols = slice(hd * HEAD_DIM, (hd + 1) * HEAD_DIM)
            y = _rms(o, dnw_ref[...]) * zs_buf[pl.ds(r0, SPAN), cols]
            yb_ref[pl.ds(r0, SPAN), cols] = y.astype(BF16)
        return carry

    lax.fori_loop(0, n_span, span_body, 0)


def _delta(x, pre_norm_w, w_qkv, w_ab, w_zd, conv_w, a_log, dt_bias, dn_norm_w):
    b, s, d = x.shape
    tm = TILE_DELTA
    n_span = tm // SPAN
    const2 = lambda i, j: (0, 0)
    return pl.pallas_call(
        _delta_kernel,
        grid=(b, s // tm),
        in_specs=[
            pl.BlockSpec((None, tm, d), lambda i, j: (i, j, 0)),
            pl.BlockSpec((1, d), const2),
            pl.BlockSpec(w_qkv.shape, lambda i, j: (0, 0, 0)),
            pl.BlockSpec(w_ab.shape, const2),
            pl.BlockSpec(w_zd.shape, const2),
            pl.BlockSpec(conv_w.shape, lambda i, j: (0, 0, 0)),
            pl.BlockSpec((1, LANES), const2),
            pl.BlockSpec((1, LANES), const2),
            pl.BlockSpec((1, HEAD_DIM), const2),
        ],
        out_specs=pl.BlockSpec((None, tm, DN_WIDTH), lambda i, j: (i, j, 0)),
        out_shape=jax.ShapeDtypeStruct((b, s, DN_WIDTH), BF16),
        scratch_shapes=[
            pltpu.VMEM((CONV_HALO, 3 * DN_WIDTH), F32),
            pltpu.VMEM((DN_HEADS, HEAD_DIM, HEAD_DIM), F32),
            pltpu.VMEM((tm, DN_WIDTH), F32),
            pltpu.VMEM((tm, LANES), F32),
            pltpu.VMEM((RAW_SLOTS, CONV_HALO + tm, GROUP_W), F32),
            pltpu.VMEM((DN_HEADS, n_span, 2 * SPAN, HEAD_DIM), BF16),
            pltpu.VMEM((DN_HEADS, tm, HEAD_DIM), BF16),
            pltpu.VMEM((DN_HEADS, tm, 2 * HEAD_DIM), BF16),
            pltpu.VMEM((DN_HEADS, tm // CHUNK, SPAN, HEAD_DIM), BF16),
            pltpu.VMEM((DN_HEADS, n_span, HEAD_DIM, SPAN), BF16),
            pltpu.VMEM((DN_HEADS, n_span, SPAN, SPAN), BF16),
            pltpu.VMEM((DN_HEADS, tm, HEAD_DIM), F32),
            pltpu.VMEM((2 * HEAD_GROUP, n_span, 2 * SPAN, SPAN), BF16),
            pltpu.VMEM((2 * HEAD_GROUP, n_span, SPAN, SPAN), F32),
            pltpu.VMEM((DN_HEADS, SPAN, HEAD_DIM), BF16),
            pltpu.VMEM((DN_HEADS, SPAN, HEAD_DIM), F32),
        ],
        compiler_params=pltpu.CompilerParams(
            dimension_semantics=("arbitrary", "arbitrary"), vmem_limit_bytes=VMEM_LIMIT_BYTES),
        name="delta_branch",
    )(x, pre_norm_w, w_qkv, w_ab, w_zd, conv_w, a_log, dt_bias, dn_norm_w)


def _merge_kernel(x_ref, nw_ref, wg_ref, ya_ref, yb_ref, yc_ref, wpa_ref, wpb_ref, wpc_ref,
                  wout_ref, pw_ref, o_ref):
    x = x_ref[...]
    h = _rms(x, nw_ref[...]).astype(BF16)
    y = None
    for br, (y_ref, w_ref) in enumerate(((ya_ref, wpa_ref), (yb_ref, wpb_ref), (yc_ref, wpc_ref))):
        gate = _sigmoid(_dot(h, wg_ref[:, br * D_MODEL:(br + 1) * D_MODEL]))
        term = gate * _dot(y_ref[...], w_ref[...])
        y = term if y is None else y + term
    out = _dot(y.astype(BF16), wout_ref[...])
    o_ref[...] = x + _rms(out, pw_ref[...])


def _merge(x2, pre_norm_w, w_gate, ya, yb, yc, w_pa, w_pb, w_pc, w_out, post_norm_w):
    n, d = x2.shape
    tm = TILE_MERGE
    const = lambda i: (0, 0)
    row = lambda i: (i, 0)
    return pl.pallas_call(
        _merge_kernel,
        grid=(n // tm,),
        in_specs=[
            pl.BlockSpec((tm, d), row),
            pl.BlockSpec((1, d), const),
            pl.BlockSpec(w_gate.shape, const),
            pl.BlockSpec((tm, POOL_WIDTH), row),
            pl.BlockSpec((tm, DN_WIDTH), row),
            pl.BlockSpec((tm, MEM_WIDTH), row),
            pl.BlockSpec(w_pa.shape, const),
            pl.BlockSpec(w_pb.shape, const),
            pl.BlockSpec(w_pc.shape, const),
            pl.BlockSpec(w_out.shape, const),
            pl.BlockSpec((1, d), const),
        ],
        out_specs=pl.BlockSpec((tm, d), row),
        out_shape=jax.ShapeDtypeStruct((n, d), F32),
        compiler_params=pltpu.CompilerParams(
            dimension_semantics=("arbitrary",), vmem_limit_bytes=VMEM_LIMIT_BYTES),
        name="merge_out",
    )(x2, pre_norm_w, w_gate, ya, yb, yc, w_pa, w_pb, w_pc, w_out, post_norm_w)


def _layer(x, mem, pre_norm_w, mem_norm_w, w_in, conv_w, a_log, dt_bias, dn_norm_w, pool_mix_w,
           pool_scale, w_mem_kv, w_proj_pool, w_proj_delta, w_proj_mem, w_out, post_norm_w):
    b, s, d = x.shape
    assert d == D_MODEL and s % TILE_LOCAL == 0 and s % TILE_DELTA == 0
    assert (b * s) % TILE_MERGE == 0 and TILE_DELTA % SPAN == 0
    row = lambda v: v.reshape(1, -1).astype(F32)
    pad_lanes = lambda v: jnp.pad(v.astype(F32), (0, LANES - v.shape[0])).reshape(1, LANES)

    w_in_b = w_in.astype(BF16)
    w_loc = jnp.concatenate(
        [w_in_b[:, OFF_XA:OFF_Q], w_in_b[:, OFF_QM:OFF_GATE]], axis=1)
    n_blk = 3 * DN_WIDTH // GROUP_W
    w_qkv = w_in_b[:, OFF_Q:OFF_A].reshape(d, n_blk, GROUP_W).transpose(1, 0, 2)
    conv_heads = conv_w.astype(F32).reshape(CONV_WIDTH, n_blk, GROUP_W).transpose(1, 0, 2)
    w_ab = jnp.pad(w_in_b[:, OFF_A:OFF_ZD], ((0, 0), (0, LANES - 2 * DN_HEADS)))
    w_zd = w_in_b[:, OFF_ZD:OFF_QM]
    w_gate = w_in_b[:, OFF_GATE:]

    kt, v = _mem_kv(mem, row(mem_norm_w), w_mem_kv[:, :MEM_WIDTH].T.astype(BF16),
                    w_mem_kv[:, MEM_WIDTH:].astype(BF16))
    ya, yc = _local(x, row(pre_norm_w), w_loc, pool_mix_w.astype(BF16), row(pool_scale), kt, v)
    yb = _delta(x, row(pre_norm_w), w_qkv, w_ab, w_zd, conv_heads, pad_lanes(a_log),
                pad_lanes(dt_bias), row(dn_norm_w))
    out = _merge(x.reshape(b * s, d), row(pre_norm_w), w_gate, ya.reshape(b * s, -1),
                 yb.reshape(b * s, -1), yc.reshape(b * s, -1), w_proj_pool.astype(BF16),
                 w_proj_delta.astype(BF16), w_proj_mem.astype(BF16), w_out.astype(BF16),
                 row(post_norm_w))
    return out.reshape(b, s, d)


def kernel(x, mem, pre_norm_w, mem_norm_w, w_in, conv_w, a_log, dt_bias, dn_norm_w, pool_mix_w,
           pool_scale, w_mem_kv, w_proj_pool, w_proj_delta, w_proj_mem, w_out, post_norm_w):
    for l in range(pre_norm_w.shape[0]):
        x = _layer(x, mem, pre_norm_w[l], mem_norm_w[l], w_in[l], conv_w[l], a_log[l], dt_bias[l],
                   dn_norm_w[l], pool_mix_w[l], pool_scale[l], w_mem_kv[l], w_proj_pool[l],
                   w_proj_delta[l], w_proj_mem[l], w_out[l], post_norm_w[l])
    return x
```

```python
import functools

import jax
import jax.numpy as jnp
from jax import lax
from jax.experimental import pallas as pl
from jax.experimental.pallas import tpu as pltpu

F32 = jnp.float32
BF16 = jnp.bfloat16

D_MODEL = 1024
HEAD_DIM = 128
DN_HEADS = D_MODEL // HEAD_DIM
DN_WIDTH = DN_HEADS * HEAD_DIM
POOL_WINDOWS = (2, 4, 8, 16)
POOL_GROUPS = len(POOL_WINDOWS)
POOL_WIDTH = D_MODEL // 2
POOL_GROUP_DIM = POOL_WIDTH // POOL_GROUPS
MEM_HEADS = 4
MEM_WIDTH = D_MODEL // 2
MEM_HEAD_DIM = MEM_WIDTH // MEM_HEADS
CONV_WIDTH = 4
CHUNK = 64
N_BRANCH = 3
EPS = 1e-6

OFF_XA = 0
OFF_ZA = OFF_XA + POOL_WIDTH
OFF_Q = OFF_ZA + POOL_WIDTH
OFF_K = OFF_Q + DN_WIDTH
OFF_V = OFF_K + DN_WIDTH
OFF_A = OFF_V + DN_WIDTH
OFF_B = OFF_A + DN_HEADS
OFF_ZD = OFF_B + DN_HEADS
OFF_QM = OFF_ZD + DN_WIDTH
OFF_ZM = OFF_QM + MEM_WIDTH
OFF_GATE = OFF_ZM + MEM_WIDTH

LANES = 128
SUBLANES = 8
SPAN = 2 * CHUNK
POOL_HALO = 16
CONV_HALO = SUBLANES
HEAD_GROUP = 2
GROUP_W = HEAD_GROUP * HEAD_DIM
RAW_SLOTS = 4
VMEM_LIMIT_BYTES = 56 * 1024 * 1024

TILE_LOCAL = 512
TILE_DELTA = 512
TILE_MERGE = 512


def _rms(x, w):
    return x * lax.rsqrt(jnp.mean(x * x, axis=-1, keepdims=True) + EPS) * w


def _sigmoid(x):
    return 1.0 / (1.0 + jnp.exp(-x))


def _silu(x):
    return x * _sigmoid(x)


def _dot(a, b):
    return jnp.dot(a, b, preferred_element_type=F32)


def _dot_nt(a, b):
    return lax.dot_general(a, b, (((1,), (1,)), ((), ())), preferred_element_type=F32)


def _mem_kv_kernel(mem_ref, nw_ref, wkv_ref, kt_ref, v_ref):
    mn = _rms(mem_ref[...], nw_ref[...]).astype(BF16)
    kv = _dot(mn, wkv_ref[...])
    kt_ref[...] = kv[:, :MEM_WIDTH].T.astype(BF16)
    v_ref[...] = kv[:, MEM_WIDTH:].astype(BF16)


def _mem_kv(mem, mem_norm_w, w_kv):
    b, m, d = mem.shape
    return pl.pallas_call(
        _mem_kv_kernel,
        grid=(b,),
        in_specs=[
            pl.BlockSpec((None, m, d), lambda i: (i, 0, 0)),
            pl.BlockSpec((1, d), lambda i: (0, 0)),
            pl.BlockSpec((d, 2 * MEM_WIDTH), lambda i: (0, 0)),
        ],
        out_specs=[
            pl.BlockSpec((None, MEM_WIDTH, m), lambda i: (i, 0, 0)),
            pl.BlockSpec((None, m, MEM_WIDTH), lambda i: (i, 0, 0)),
        ],
        out_shape=[
            jax.ShapeDtypeStruct((b, MEM_WIDTH, m), BF16),
            jax.ShapeDtypeStruct((b, m, MEM_WIDTH), BF16),
        ],
        compiler_params=pltpu.CompilerParams(
            dimension_semantics=("arbitrary",), vmem_limit_bytes=VMEM_LIMIT_BYTES),
        name="mem_kv",
    )(mem, mem_norm_w, w_kv)


def _local_kernel(x_ref, nw_ref, w_ref, mix_ref, scale_ref, kt_ref, v_ref, ya_ref, yc_ref,
                  halo_ref):
    tm = x_ref.shape[0]
    s = pl.program_id(1)

    @pl.when(s == 0)
    def _():
        halo_ref[...] = jnp.zeros_like(halo_ref)

    h = _rms(x_ref[...], nw_ref[...]).astype(BF16)
    proj = _dot(h, w_ref[...])
    xa = proj[:, 0:POOL_WIDTH]
    za = proj[:, POOL_WIDTH:2 * POOL_WIDTH]
    qm = proj[:, 2 * POOL_WIDTH:2 * POOL_WIDTH + MEM_WIDTH]
    zm = proj[:, 2 * POOL_WIDTH + MEM_WIDTH:]

    ext = jnp.concatenate([halo_ref[...], xa], axis=0)
    halo_ref[...] = xa[tm - POOL_HALO:, :]
    pos = (s * tm + 1 + lax.broadcasted_iota(jnp.int32, (tm, 1), 0)).astype(F32)
    mixed = []
    for g, window in enumerate(POOL_WINDOWS):
        cols = slice(g * POOL_GROUP_DIM, (g + 1) * POOL_GROUP_DIM)
        acc = ext[:, cols]
        width = 1
        while width < window:
            acc = acc + pltpu.roll(acc, width, 0)
            width *= 2
        mean = acc[POOL_HALO:, :] / jnp.minimum(pos, float(window))
        p = mean - xa[:, cols]
        mixed.append(_dot(p.astype(BF16), mix_ref[g]))
    m = jnp.concatenate(mixed, axis=1) * scale_ref[...]
    ya_ref[...] = (m * _silu(za)).astype(BF16)

    heads = []
    for hh in range(MEM_HEADS):
        cols = slice(hh * MEM_HEAD_DIM, (hh + 1) * MEM_HEAD_DIM)
        sc = _dot(qm[:, cols].astype(BF16), kt_ref[cols, :]) * (MEM_HEAD_DIM ** -0.5)
        e = jnp.exp(sc - jnp.max(sc, axis=-1, keepdims=True))
        o = _dot(e.astype(BF16), v_ref[:, cols])
        heads.append(o / jnp.sum(e, axis=-1, keepdims=True))
    yc_ref[...] = (jnp.concatenate(heads, axis=1) * _silu(zm)).astype(BF16)


def _local(x, pre_norm_w, w_loc, mix_w, pool_scale, kt, v):
    b, s, d = x.shape
    tm = TILE_LOCAL
    m = kt.shape[-1]
    const2 = lambda i, j: (0, 0)
    return pl.pallas_call(
        _local_kernel,
        grid=(b, s // tm),
        in_specs=[
            pl.BlockSpec((None, tm, d), lambda i, j: (i, j, 0)),
            pl.BlockSpec((1, d), const2),
            pl.BlockSpec(w_loc.shape, const2),
            pl.BlockSpec(mix_w.shape, lambda i, j: (0, 0, 0)),
            pl.BlockSpec((1, POOL_WIDTH), const2),
            pl.BlockSpec((None, MEM_WIDTH, m), lambda i, j: (i, 0, 0)),
            pl.BlockSpec((None, m, MEM_WIDTH), lambda i, j: (i, 0, 0)),
        ],
        out_specs=[
            pl.BlockSpec((None, tm, POOL_WIDTH), lambda i, j: (i, j, 0)),
            pl.BlockSpec((None, tm, MEM_WIDTH), lambda i, j: (i, j, 0)),
        ],
        out_shape=[
            jax.ShapeDtypeStruct((b, s, POOL_WIDTH), BF16),
            jax.ShapeDtypeStruct((b, s, MEM_WIDTH), BF16),
        ],
        scratch_shapes=[pltpu.VMEM((POOL_HALO, POOL_WIDTH), F32)],
        compiler_params=pltpu.CompilerParams(
            dimension_semantics=("arbitrary", "arbitrary"), vmem_limit_bytes=VMEM_LIMIT_BYTES),
        name="local_branches",
    )(x, pre_norm_w, w_loc, mix_w, pool_scale, kt, v)


def _delta_kernel(tiles_per_seq, x_ref, nw_ref, wqkv_ref, wab_ref, wzd_ref, convw_ref, alog_ref,
                  dtb_ref, dnw_ref, yb_ref, halo_ref, state_ref, zs_buf, e_buf, raw_buf,
                  kq_lhs, kn_b, rhs_b, wq_b, kdt_b, aqk_b, u_f, tp_b, t_f, vn_b, oi_f):
    tm = x_ref.shape[0]
    n_span = tm // SPAN
    n_chunk = tm // CHUNK
    t = pl.program_id(0)
    n_tiles = pl.num_programs(0) - 1
    heads = range(DN_HEADS)
    top = slice(0, SPAN)
    bot = slice(SPAN, 2 * SPAN)

    cur = lax.rem(t, 2)
    prv = 1 - cur
    two_slot = (wq_b, kdt_b, aqk_b, u_f, zs_buf, e_buf)
    wq_w, kdt_w, aqk_w, u_w, zs_w, e_w = (r.at[cur] for r in two_slot)
    wq_r, kdt_r, aqk_r, u_r, zs_r, e_r = (r.at[prv] for r in two_slot)

    @pl.when(t == 0)
    def _():
        for r in two_slot:
            r[1] = jnp.zeros(r.shape[1:], r.dtype)

    @pl.when(lax.rem(t, tiles_per_seq) == 0)
    def _():
        halo_ref[...] = jnp.zeros_like(halo_ref)

    @pl.when((t == 0) | (lax.rem(t + tiles_per_seq - 1, tiles_per_seq) == 0))
    def _():
        state_ref[...] = jnp.zeros_like(state_ref)

    zeros_chunk = jnp.zeros((CHUNK, HEAD_DIM), BF16)

    def rec_apply(sp, c):
        rows = slice(c * CHUNK, (c + 1) * CHUNK)
        trows = slice(sp * SPAN + c * CHUNK, sp * SPAN + (c + 1) * CHUNK)
        for hd in heads:
            r = _dot(wq_r[hd, 2 * sp + c], state_ref[hd].astype(BF16))
            vn_b[hd, rows, :] = (u_r[hd, trows, :] - r[:CHUNK, :]).astype(BF16)
            oi_f[hd, rows, :] = r[CHUNK:, :]

    def rec_update(sp, c):
        rows = slice(c * CHUNK, (c + 1) * CHUNK)
        last = sp * SPAN + (c + 1) * CHUNK - 1
        eg = e_r[last:last + 1, :]
        for hd in heads:
            padded = [zeros_chunk, zeros_chunk]
            padded[c] = vn_b[hd, rows, :]
            upd = _dot(kdt_r[hd, sp], jnp.concatenate(padded, axis=0))
            state_ref[hd] = state_ref[hd] * eg[:, hd:hd + 1] + upd

    def rec_output(sp, _):
        trows = slice(sp * SPAN, (sp + 1) * SPAN)
        for hd in heads:
            o = oi_f[hd] + _dot(aqk_r[hd, sp], vn_b[hd])
            cols = slice(hd * HEAD_DIM, (hd + 1) * HEAD_DIM)
            y = _rms(o, dnw_ref[...]) * zs_r[trows, cols]
            yb_ref[trows, cols] = y.astype(BF16)

    rec_steps = []
    for sp in range(n_span):
        rec_steps += [(rec_apply, sp, 0), (rec_update, sp, 0), (rec_apply, sp, 1),
                      (rec_update, sp, 1), (rec_output, sp, 0)]

    @pl.when(t == n_tiles)
    def _():
        for fn, sp, c in rec_steps:
            fn(sp, c)

    @pl.when(t < n_tiles)
    def _():
        _delta_tile(x_ref, nw_ref, wqkv_ref, wab_ref, wzd_ref, convw_ref, alog_ref, dtb_ref,
                    halo_ref, raw_buf, kq_lhs, kn_b, rhs_b, tp_b, t_f,
                    wq_w, kdt_w, aqk_w, u_w, zs_w, e_w, rec_steps)


def _delta_tile(x_ref, nw_ref, wqkv_ref, wab_ref, wzd_ref, convw_ref, alog_ref, dtb_ref,
                halo_ref, raw_buf, kq_lhs, kn_b, rhs_b, tp_b, t_f,
                wq_b, kdt_b, aqk_b, u_f, zs_buf, e_buf, rec_steps):
    tm = x_ref.shape[0]
    n_span = tm // SPAN
    n_chunk = tm // CHUNK
    top = slice(0, SPAN)
    bot = slice(SPAN, 2 * SPAN)
    pending = list(rec_steps)

    def emit_rec_step():
        if pending:
            fn, sp, c = pending.pop(0)
            fn(sp, c)

    h = _rms(x_ref[...], nw_ref[...]).astype(BF16)

    ab = _dot(h, wab_ref[...])
    sp_in = ab + dtb_ref[...]
    softplus = jnp.maximum(sp_in, 0.0) + jnp.log1p(jnp.exp(-jnp.abs(sp_in)))
    g = -jnp.exp(alog_ref[...]) * softplus
    beta = _sigmoid(ab)

    row_in_chunk = lax.broadcasted_iota(jnp.int32, (tm, LANES), 0) % CHUNK
    gc = g
    shift = 1
    while shift < CHUNK:
        gc = gc + jnp.where(row_in_chunk >= shift, pltpu.roll(gc, shift, 0), 0.0)
        shift *= 2
    e = jnp.exp(gc)
    e_buf[...] = e
    glast = jnp.concatenate(
        [jnp.broadcast_to(gc[(c + 1) * CHUNK - 1:(c + 1) * CHUNK, :], (CHUNK, LANES))
         for c in range(n_chunk)], axis=0)
    kdf = jnp.exp(glast - gc)

    ri = lax.broadcasted_iota(jnp.int32, (SPAN, SPAN), 0)
    ci = lax.broadcasted_iota(jnp.int32, (SPAN, SPAN), 1)
    same_chunk = (ri // CHUNK) == (ci // CHUNK)
    tril = same_chunk & (ri >= ci)
    stril = same_chunk & (ri > ci)
    eye = (ri == ci).astype(F32)
    gct = [gc[i * SPAN:(i + 1) * SPAN, :].T for i in range(n_span)]

    n_raw = raw_buf.shape[0]

    def conv_silu(which, gi):
        blk = which * (DN_HEADS // HEAD_GROUP) + gi
        slot = blk % n_raw
        bcols = slice(blk * GROUP_W, (blk + 1) * GROUP_W)
        raw = _dot(h, wqkv_ref[:, bcols])
        raw_buf[slot, 0:CONV_HALO, :] = halo_ref[:, bcols]
        raw_buf[slot, CONV_HALO:, :] = raw
        halo_ref[:, bcols] = raw[tm - CONV_HALO:, :]
        w = convw_ref[:, bcols]
        acc = raw * w[CONV_WIDTH - 1:CONV_WIDTH, :]
        for back in range(1, CONV_WIDTH):
            tap = CONV_WIDTH - 1 - back
            start = CONV_HALO - back
            acc = acc + raw_buf[slot, start:start + tm, :] * w[tap:tap + 1, :]
        out = _silu(acc)
        return [out[:, j * HEAD_DIM:(j + 1) * HEAD_DIM] for j in range(HEAD_GROUP)]

    def prep_k(gi):
        for j, k in enumerate(conv_silu(1, gi)):
            hd = gi * HEAD_GROUP + j
            kn = k * lax.rsqrt(jnp.sum(k * k, axis=-1, keepdims=True) + EPS)
            kb = kn * beta[:, DN_HEADS + hd:DN_HEADS + hd + 1]
            kb_b = kb.astype(BF16)
            kd = kn * kdf[:, hd:hd + 1]
            kn_b[hd] = kn.astype(BF16)
            rhs_b[hd, :, HEAD_DIM:] = (kb * e[:, hd:hd + 1]).astype(BF16)
            for i in range(n_span):
                rows = slice(i * SPAN, (i + 1) * SPAN)
                kq_lhs[hd, i, top, :] = kb_b[rows, :]
                kdt_b[hd, i] = kd[rows, :].T.astype(BF16)

    def prep_q(gi):
        for j, q in enumerate(conv_silu(0, gi)):
            hd = gi * HEAD_GROUP + j
            qn = q * lax.rsqrt(jnp.sum(q * q, axis=-1, keepdims=True) + EPS) * (HEAD_DIM ** -0.5)
            qn_b = qn.astype(BF16)
            qg = (qn * e[:, hd:hd + 1]).astype(BF16)
            for i in range(n_span):
                kq_lhs[hd, i, bot, :] = qn_b[i * SPAN:(i + 1) * SPAN, :]
            for c in range(n_chunk):
                wq_b[hd, c, CHUNK:, :] = qg[c * CHUNK:(c + 1) * CHUNK, :]

    def prep_v(gi):
        for j, v in enumerate(conv_silu(2, gi)):
            hd = gi * HEAD_GROUP + j
            rhs_b[hd, :, 0:HEAD_DIM] = (v * beta[:, DN_HEADS + hd:DN_HEADS + hd + 1]).astype(BF16)

    def gate_piece(blk):
        cols = slice(blk * GROUP_W, (blk + 1) * GROUP_W)
        zs_buf[:, cols] = _silu(_dot(h, wzd_ref[:, cols]))

    def prep_pieces(gi):
        return [(prep_k, gi), (prep_q, gi), (prep_v, gi)]

    def stage_score(items):
        for hd, i, sl in items:
            rows = slice(i * SPAN, (i + 1) * SPAN)
            diff = gc[rows, hd:hd + 1] - gct[i][hd:hd + 1, :]
            decay = jnp.where(tril, jnp.exp(jnp.where(tril, diff, 0.0)), 0.0)
            kq = _dot_nt(kq_lhs[hd, i], kn_b[hd, rows, :])
            a = jnp.where(stril, kq[top, :] * decay, 0.0)
            aqk_b[hd, i] = (kq[bot, :] * decay).astype(BF16)
            t0 = eye - a
            t_f[sl, i] = t0
            tp_b[sl, i, top, :] = t0.astype(BF16)
            tp_b[sl, i, bot, :] = a.astype(BF16)

    def stage_square(items):
        for hd, i, sl in items:
            a_b = tp_b[sl, i, bot, :]
            tp_b[sl, i, bot, :] = _dot(a_b, a_b).astype(BF16)

    def stage_combined(items):
        for hd, i, sl in items:
            r = _dot(tp_b[sl, i], tp_b[sl, i, bot, :])
            t_new = t_f[sl, i] + r[top, :]
            t_f[sl, i] = t_new
            tp_b[sl, i, top, :] = t_new.astype(BF16)
            tp_b[sl, i, bot, :] = r[bot, :].astype(BF16)

    def stage_final(items):
        for hd, i, sl in items:
            t_new = t_f[sl, i] + _dot(tp_b[sl, i, top, :], tp_b[sl, i, bot, :])
            tp_b[sl, i, top, :] = t_new.astype(BF16)

    def stage_uw(items):
        for hd, i, sl in items:
            rows = slice(i * SPAN, (i + 1) * SPAN)
            uw = _dot(tp_b[sl, i, top, :], rhs_b[hd, rows, :])
            u_f[hd, rows, :] = uw[:, :HEAD_DIM]
            w = uw[:, HEAD_DIM:].astype(BF16)
            for c in range(2):
                wq_b[hd, 2 * i + c, 0:CHUNK, :] = w[c * CHUNK:(c + 1) * CHUNK, :]

    n_combined = 0
    order = 2
    while 2 * order < CHUNK:
        n_combined += 1
        order *= 2
    stages = [stage_score, stage_square] + [stage_combined] * n_combined + [stage_final, stage_uw]

    n_slot = tp_b.shape[0]
    groups = [range(g0, g0 + HEAD_GROUP) for g0 in range(0, DN_HEADS, HEAD_GROUP)]
    emit_rec_step()
    for fn, arg in prep_pieces(0):
        fn(arg)
        emit_rec_step()
    for gi, group in enumerate(groups):
        items = [(hd, i, hd % n_slot) for hd in group for i in range(n_span)]
        if gi + 1 < len(groups):
            fillers = prep_pieces(gi + 1)
        else:
            fillers = [(gate_piece, blk) for blk in range(DN_WIDTH // GROUP_W)]
        every = len(stages) // len(fillers)
        for n, stage in enumerate(stages):
            stage(items)
            if n % every == 0 and n // every < len(fillers):
                fn, arg = fillers[n // every]
                fn(arg)
            emit_rec_step()
    while pending:
        emit_rec_step()


def _delta(x, pre_norm_w, w_qkv, w_ab, w_zd, conv_w, a_log, dt_bias, dn_norm_w):
    b, s, d = x.shape
    tm = TILE_DELTA
    n_span = tm // SPAN
    tiles_per_seq = s // tm
    n_tiles = b * tiles_per_seq
    const = lambda t: (0, 0)

    def in_tile(t):
        ta = jnp.minimum(t, n_tiles - 1)
        return (ta // tiles_per_seq, ta % tiles_per_seq, 0)

    def out_tile(t):
        tb = jnp.maximum(t - 1, 0)
        return (tb // tiles_per_seq, tb % tiles_per_seq, 0)

    return pl.pallas_call(
        functools.partial(_delta_kernel, tiles_per_seq),
        grid=(n_tiles + 1,),
        in_specs=[
            pl.BlockSpec((None, tm, d), in_tile),
            pl.BlockSpec((1, d), const),
            pl.BlockSpec(w_qkv.shape, const),
            pl.BlockSpec(w_ab.shape, const),
            pl.BlockSpec(w_zd.shape, const),
            pl.BlockSpec(conv_w.shape, const),
            pl.BlockSpec((1, LANES), const),
            pl.BlockSpec((1, LANES), const),
            pl.BlockSpec((1, HEAD_DIM), const),
        ],
        out_specs=pl.BlockSpec((None, tm, DN_WIDTH), out_tile),
        out_shape=jax.ShapeDtypeStruct((b, s, DN_WIDTH), BF16),
        scratch_shapes=[
            pltpu.VMEM((CONV_HALO, 3 * DN_WIDTH), F32),
            pltpu.VMEM((DN_HEADS, HEAD_DIM, HEAD_DIM), F32),
            pltpu.VMEM((2, tm, DN_WIDTH), F32),
            pltpu.VMEM((2, tm, LANES), F32),
            pltpu.VMEM((RAW_SLOTS, CONV_HALO + tm, GROUP_W), F32),
            pltpu.VMEM((DN_HEADS, n_span, 2 * SPAN, HEAD_DIM), BF16),
            pltpu.VMEM((DN_HEADS, tm, HEAD_DIM), BF16),
            pltpu.VMEM((DN_HEADS, tm, 2 * HEAD_DIM), BF16),
            pltpu.VMEM((2, DN_HEADS, tm // CHUNK, SPAN, HEAD_DIM), BF16),
            pltpu.VMEM((2, DN_HEADS, n_span, HEAD_DIM, SPAN), BF16),
            pltpu.VMEM((2, DN_HEADS, n_span, SPAN, SPAN), BF16),
            pltpu.VMEM((2, DN_HEADS, tm, HEAD_DIM), F32),
            pltpu.VMEM((2 * HEAD_GROUP, n_span, 2 * SPAN, SPAN), BF16),
            pltpu.VMEM((2 * HEAD_GROUP, n_span, SPAN, SPAN), F32),
            pltpu.VMEM((DN_HEADS, SPAN, HEAD_DIM), BF16),
            pltpu.VMEM((DN_HEADS, SPAN, HEAD_DIM), F32),
        ],
        compiler_params=pltpu.CompilerParams(
            dimension_semantics=("arbitrary",), vmem_limit_bytes=VMEM_LIMIT_BYTES),
        name="delta_branch",
    )(x, pre_norm_w, w_qkv, w_ab, w_zd, conv_w, a_log, dt_bias, dn_norm_w)


def _merge_kernel(x_ref, nw_ref, wg_ref, ya_ref, yb_ref, yc_ref, wpa_ref, wpb_ref, wpc_ref,
                  wout_ref, pw_ref, o_ref):
    x = x_ref[...]
    h = _rms(x, nw_ref[...]).astype(BF16)
    y = None
    for br, (y_ref, w_ref) in enumerate(((ya_ref, wpa_ref), (yb_ref, wpb_ref), (yc_ref, wpc_ref))):
        gate = _sigmoid(_dot(h, wg_ref[:, br * D_MODEL:(br + 1) * D_MODEL]))
        term = gate * _dot(y_ref[...], w_ref[...])
        y = term if y is None else y + term
    out = _dot(y.astype(BF16), wout_ref[...])
    o_ref[...] = x + _rms(out, pw_ref[...])


def _merge(x2, pre_norm_w, w_gate, ya, yb, yc, w_pa, w_pb, w_pc, w_out, post_norm_w):
    n, d = x2.shape
    tm = TILE_MERGE
    const = lambda i: (0, 0)
    row = lambda i: (i, 0)
    return pl.pallas_call(
        _merge_kernel,
        grid=(n // tm,),
        in_specs=[
            pl.BlockSpec((tm, d), row),
            pl.BlockSpec((1, d), const),
            pl.BlockSpec(w_gate.shape, const),
            pl.BlockSpec((tm, POOL_WIDTH), row),
            pl.BlockSpec((tm, DN_WIDTH), row),
            pl.BlockSpec((tm, MEM_WIDTH), row),
            pl.BlockSpec(w_pa.shape, const),
            pl.BlockSpec(w_pb.shape, const),
            pl.BlockSpec(w_pc.shape, const),
            pl.BlockSpec(w_out.shape, const),
            pl.BlockSpec((1, d), const),
        ],
        out_specs=pl.BlockSpec((tm, d), row),
        out_shape=jax.ShapeDtypeStruct((n, d), F32),
        compiler_params=pltpu.CompilerParams(
            dimension_semantics=("arbitrary",), vmem_limit_bytes=VMEM_LIMIT_BYTES),
        name="merge_out",
    )(x2, pre_norm_w, w_gate, ya, yb, yc, w_pa, w_pb, w_pc, w_out, post_norm_w)


def _layer(x, mem, pre_norm_w, mem_norm_w, w_in, conv_w, a_log, dt_bias, dn_norm_w, pool_mix_w,
           pool_scale, w_mem_kv, w_proj_pool, w_proj_delta, w_proj_mem, w_out, post_norm_w):
    b, s, d = x.shape
    assert d == D_MODEL and s % TILE_LOCAL == 0 and s % TILE_DELTA == 0
    assert (b * s) % TILE_MERGE == 0 and TILE_DELTA % SPAN == 0
    row = lambda v: v.reshape(1, -1).astype(F32)
    pad_lanes = lambda v: jnp.pad(v.astype(F32), (0, LANES - v.shape[0])).reshape(1, LANES)

    w_loc = jnp.concatenate(
        [w_in[:, OFF_XA:OFF_Q], w_in[:, OFF_QM:OFF_GATE]], axis=1).astype(BF16)
    w_qkv = w_in[:, OFF_Q:OFF_A].astype(BF16)
    w_ab = jnp.pad(w_in[:, OFF_A:OFF_ZD], ((0, 0), (0, LANES - 2 * DN_HEADS))).astype(BF16)
    w_zd = w_in[:, OFF_ZD:OFF_QM].astype(BF16)
    w_gate = w_in[:, OFF_GATE:].astype(BF16)

    kt, v = _mem_kv(mem, row(mem_norm_w), w_mem_kv.astype(BF16))
    ya, yc = _local(x, row(pre_norm_w), w_loc, pool_mix_w.astype(BF16), row(pool_scale), kt, v)
    yb = _delta(x, row(pre_norm_w), w_qkv, w_ab, w_zd, conv_w.astype(F32), pad_lanes(a_log),
                pad_lanes(dt_bias), row(dn_norm_w))
    out = _merge(x.reshape(b * s, d), row(pre_norm_w), w_gate, ya.reshape(b * s, -1),
                 yb.reshape(b * s, -1), yc.reshape(b * s, -1), w_proj_pool.astype(BF16),
                 w_proj_delta.astype(BF16), w_proj_mem.astype(BF16), w_out.astype(BF16),
                 row(post_norm_w))
    return out.reshape(b, s, d)


def kernel(x, mem, pre_norm_w, mem_norm_w, w_in, conv_w, a_log, dt_bias, dn_norm_w, pool_mix_w,
           pool_scale, w_mem_kv, w_proj_pool, w_proj_delta, w_proj_mem, w_out, post_norm_w):
    for l in range(pre_norm_w.shape[0]):
        x = _layer(x, mem, pre_norm_w[l], mem_norm_w[l], w_in[l], conv_w[l], a_log[l], dt_bias[l],
                   dn_norm_w[l], pool_mix_w[l], pool_scale[l], w_mem_kv[l], w_proj_pool[l],
                   w_proj_delta[l], w_proj_mem[l], w_out[l], post_norm_w[l])
    return x
```

```python
import jax
import jax.numpy as jnp
from jax import lax
from jax.experimental import pallas as pl
from jax.experimental.pallas import tpu as pltpu

F32 = jnp.float32
BF16 = jnp.bfloat16

D_MODEL = 1024
HEAD_DIM = 128
DN_HEADS = D_MODEL // HEAD_DIM
DN_WIDTH = DN_HEADS * HEAD_DIM
POOL_WINDOWS = (2, 4, 8, 16)
POOL_GROUPS = len(POOL_WINDOWS)
POOL_WIDTH = D_MODEL // 2
POOL_GROUP_DIM = POOL_WIDTH // POOL_GROUPS
MEM_HEADS = 4
MEM_WIDTH = D_MODEL // 2
MEM_HEAD_DIM = MEM_WIDTH // MEM_HEADS
CONV_WIDTH = 4
CHUNK = 64
N_BRANCH = 3
EPS = 1e-6

OFF_XA = 0
OFF_ZA = OFF_XA + POOL_WIDTH
OFF_Q = OFF_ZA + POOL_WIDTH
OFF_K = OFF_Q + DN_WIDTH
OFF_V = OFF_K + DN_WIDTH
OFF_A = OFF_V + DN_WIDTH
OFF_B = OFF_A + DN_HEADS
OFF_ZD = OFF_B + DN_HEADS
OFF_QM = OFF_ZD + DN_WIDTH
OFF_ZM = OFF_QM + MEM_WIDTH
OFF_GATE = OFF_ZM + MEM_WIDTH

LANES = 128
SUBLANES = 8
SPAN = 2 * CHUNK
POOL_HALO = 16
CONV_HALO = SUBLANES
HEAD_GROUP = 2
GROUP_W = HEAD_GROUP * HEAD_DIM
RAW_SLOTS = 4
VMEM_LIMIT_BYTES = 56 * 1024 * 1024

TILE_LOCAL = 512
TILE_DELTA = 512
TILE_MERGE = 512


def _rms(x, w):
    return x * lax.rsqrt(jnp.mean(x * x, axis=-1, keepdims=True) + EPS) * w


def _sigmoid(x):
    return 1.0 / (1.0 + jnp.exp(-x))


def _silu(x):
    return x * _sigmoid(x)


def _dot(a, b):
    return jnp.dot(a, b, preferred_element_type=F32)


def _dot_nt(a, b):
    return lax.dot_general(a, b, (((1,), (1,)), ((), ())), preferred_element_type=F32)


def _mem_kv_kernel(mem_ref, nw_ref, wkv_ref, kt_ref, v_ref):
    mn = _rms(mem_ref[...], nw_ref[...]).astype(BF16)
    kv = _dot(mn, wkv_ref[...])
    kt_ref[...] = kv[:, :MEM_WIDTH].T.astype(BF16)
    v_ref[...] = kv[:, MEM_WIDTH:].astype(BF16)


def _mem_kv(mem, mem_norm_w, w_kv):
    b, m, d = mem.shape
    return pl.pallas_call(
        _mem_kv_kernel,
        grid=(b,),
        in_specs=[
            pl.BlockSpec((None, m, d), lambda i: (i, 0, 0)),
            pl.BlockSpec((1, d), lambda i: (0, 0)),
            pl.BlockSpec((d, 2 * MEM_WIDTH), lambda i: (0, 0)),
        ],
        out_specs=[
            pl.BlockSpec((None, MEM_WIDTH, m), lambda i: (i, 0, 0)),
            pl.BlockSpec((None, m, MEM_WIDTH), lambda i: (i, 0, 0)),
        ],
        out_shape=[
            jax.ShapeDtypeStruct((b, MEM_WIDTH, m), BF16),
            jax.ShapeDtypeStruct((b, m, MEM_WIDTH), BF16),
        ],
        compiler_params=pltpu.CompilerParams(
            dimension_semantics=("arbitrary",), vmem_limit_bytes=VMEM_LIMIT_BYTES),
        name="mem_kv",
    )(mem, mem_norm_w, w_kv)


def _local_kernel(x_ref, nw_ref, w_ref, mix_ref, scale_ref, kt_ref, v_ref, ya_ref, yc_ref,
                  halo_ref):
    tm = x_ref.shape[0]
    s = pl.program_id(1)

    @pl.when(s == 0)
    def _():
        halo_ref[...] = jnp.zeros_like(halo_ref)

    h = _rms(x_ref[...], nw_ref[...]).astype(BF16)
    proj = _dot_nt(h, w_ref[...])
    xa = proj[:, 0:POOL_WIDTH]
    za = proj[:, POOL_WIDTH:2 * POOL_WIDTH]
    qm = proj[:, 2 * POOL_WIDTH:2 * POOL_WIDTH + MEM_WIDTH]
    zm = proj[:, 2 * POOL_WIDTH + MEM_WIDTH:]

    ext = jnp.concatenate([halo_ref[...], xa], axis=0)
    halo_ref[...] = xa[tm - POOL_HALO:, :]
    pos = (s * tm + 1 + lax.broadcasted_iota(jnp.int32, (tm, 1), 0)).astype(F32)
    mixed = []
    for g, window in enumerate(POOL_WINDOWS):
        cols = slice(g * POOL_GROUP_DIM, (g + 1) * POOL_GROUP_DIM)
        acc = ext[:, cols]
        width = 1
        while width < window:
            acc = acc + pltpu.roll(acc, width, 0)
            width *= 2
        mean = acc[POOL_HALO:, :] / jnp.minimum(pos, float(window))
        p = mean - xa[:, cols]
        mixed.append(_dot(p.astype(BF16), mix_ref[g]))
    m = jnp.concatenate(mixed, axis=1) * scale_ref[...]
    ya_ref[...] = (m * _silu(za)).astype(BF16)

    heads = []
    for hh in range(MEM_HEADS):
        cols = slice(hh * MEM_HEAD_DIM, (hh + 1) * MEM_HEAD_DIM)
        sc = _dot(qm[:, cols].astype(BF16), kt_ref[cols, :]) * (MEM_HEAD_DIM ** -0.5)
        e = jnp.exp(sc - jnp.max(sc, axis=-1, keepdims=True))
        o = _dot(e.astype(BF16), v_ref[:, cols])
        heads.append(o / jnp.sum(e, axis=-1, keepdims=True))
    yc_ref[...] = (jnp.concatenate(heads, axis=1) * _silu(zm)).astype(BF16)


def _local(x, pre_norm_w, w_loc, mix_w, pool_scale, kt, v):
    b, s, d = x.shape
    tm = TILE_LOCAL
    m = kt.shape[-1]
    const2 = lambda i, j: (0, 0)
    return pl.pallas_call(
        _local_kernel,
        grid=(b, s // tm),
        in_specs=[
            pl.BlockSpec((None, tm, d), lambda i, j: (i, j, 0)),
            pl.BlockSpec((1, d), const2),
            pl.BlockSpec(w_loc.shape, const2),
            pl.BlockSpec(mix_w.shape, lambda i, j: (0, 0, 0)),
            pl.BlockSpec((1, POOL_WIDTH), const2),
            pl.BlockSpec((None, MEM_WIDTH, m), lambda i, j: (i, 0, 0)),
            pl.BlockSpec((None, m, MEM_WIDTH), lambda i, j: (i, 0, 0)),
        ],
        out_specs=[
            pl.BlockSpec((None, tm, POOL_WIDTH), lambda i, j: (i, j, 0)),
            pl.BlockSpec((None, tm, MEM_WIDTH), lambda i, j: (i, j, 0)),
        ],
        out_shape=[
            jax.ShapeDtypeStruct((b, s, POOL_WIDTH), BF16),
            jax.ShapeDtypeStruct((b, s, MEM_WIDTH), BF16),
        ],
        scratch_shapes=[pltpu.VMEM((POOL_HALO, POOL_WIDTH), F32)],
        compiler_params=pltpu.CompilerParams(
            dimension_semantics=("arbitrary", "arbitrary"), vmem_limit_bytes=VMEM_LIMIT_BYTES),
        name="local_branches",
    )(x, pre_norm_w, w_loc, mix_w, pool_scale, kt, v)


def _delta_kernel(x_ref, nw_ref, wqkv_ref, wab_ref, wzd_ref, wg_ref, convw_ref, alog_ref, dtb_ref,
                  dnw_ref, yb_ref, gates_ref, halo_ref, state_ref, zs_buf, e_buf, raw_buf,
                  kq_lhs, kn_b, rhs_b, wq_b, kdt_b, aqk_b, u_f, tp_b, t_f, vn_b, oi_f):
    tm = x_ref.shape[0]
    n_span = tm // SPAN
    n_chunk = tm // CHUNK
    s = pl.program_id(1)
    heads = range(DN_HEADS)
    top = slice(0, SPAN)
    bot = slice(SPAN, 2 * SPAN)

    @pl.when(s == 0)
    def _():
        halo_ref[...] = jnp.zeros_like(halo_ref)
        state_ref[...] = jnp.zeros_like(state_ref)

    h = _rms(x_ref[...], nw_ref[...]).astype(BF16)

    ab = _dot_nt(h, wab_ref[...])
    sp_in = ab + dtb_ref[...]
    softplus = jnp.maximum(sp_in, 0.0) + jnp.log1p(jnp.exp(-jnp.abs(sp_in)))
    g = -jnp.exp(alog_ref[...]) * softplus
    beta = _sigmoid(ab)

    row_in_chunk = lax.broadcasted_iota(jnp.int32, (tm, LANES), 0) % CHUNK
    gc = g
    shift = 1
    while shift < CHUNK:
        gc = gc + jnp.where(row_in_chunk >= shift, pltpu.roll(gc, shift, 0), 0.0)
        shift *= 2
    e = jnp.exp(gc)
    e_buf[...] = e
    glast = jnp.concatenate(
        [jnp.broadcast_to(gc[(c + 1) * CHUNK - 1:(c + 1) * CHUNK, :], (CHUNK, LANES))
         for c in range(n_chunk)], axis=0)
    kdf = jnp.exp(glast - gc)

    ri = lax.broadcasted_iota(jnp.int32, (SPAN, SPAN), 0)
    ci = lax.broadcasted_iota(jnp.int32, (SPAN, SPAN), 1)
    same_chunk = (ri // CHUNK) == (ci // CHUNK)
    tril = same_chunk & (ri >= ci)
    stril = same_chunk & (ri > ci)
    eye = (ri == ci).astype(F32)
    gct = [gc[i * SPAN:(i + 1) * SPAN, :].T for i in range(n_span)]

    n_raw = raw_buf.shape[0]

    def conv_silu(which, gi):
        blk = which * (DN_HEADS // HEAD_GROUP) + gi
        slot = blk % n_raw
        bcols = slice(blk * GROUP_W, (blk + 1) * GROUP_W)
        raw = _dot_nt(h, wqkv_ref[bcols, :])
        raw_buf[slot, 0:CONV_HALO, :] = halo_ref[:, bcols]
        raw_buf[slot, CONV_HALO:, :] = raw
        halo_ref[:, bcols] = raw[tm - CONV_HALO:, :]
        w = convw_ref[:, bcols]
        acc = raw * w[CONV_WIDTH - 1:CONV_WIDTH, :]
        for back in range(1, CONV_WIDTH):
            tap = CONV_WIDTH - 1 - back
            start = CONV_HALO - back
            acc = acc + raw_buf[slot, start:start + tm, :] * w[tap:tap + 1, :]
        out = _silu(acc)
        return [out[:, j * HEAD_DIM:(j + 1) * HEAD_DIM] for j in range(HEAD_GROUP)]

    def prep_k(gi):
        for j, k in enumerate(conv_silu(1, gi)):
            hd = gi * HEAD_GROUP + j
            kn = k * lax.rsqrt(jnp.sum(k * k, axis=-1, keepdims=True) + EPS)
            kb = kn * beta[:, DN_HEADS + hd:DN_HEADS + hd + 1]
            kb_b = kb.astype(BF16)
            kd = kn * kdf[:, hd:hd + 1]
            kn_b[hd] = kn.astype(BF16)
            rhs_b[hd, :, HEAD_DIM:] = (kb * e[:, hd:hd + 1]).astype(BF16)
            for i in range(n_span):
                rows = slice(i * SPAN, (i + 1) * SPAN)
                kq_lhs[hd, i, top, :] = kb_b[rows, :]
                kdt_b[hd, i] = kd[rows, :].T.astype(BF16)

    def prep_q(gi):
        for j, q in enumerate(conv_silu(0, gi)):
            hd = gi * HEAD_GROUP + j
            qn = q * lax.rsqrt(jnp.sum(q * q, axis=-1, keepdims=True) + EPS) * (HEAD_DIM ** -0.5)
            qn_b = qn.astype(BF16)
            qg = (qn * e[:, hd:hd + 1]).astype(BF16)
            for i in range(n_span):
                kq_lhs[hd, i, bot, :] = qn_b[i * SPAN:(i + 1) * SPAN, :]
            for c in range(n_chunk):
                wq_b[hd, c, CHUNK:, :] = qg[c * CHUNK:(c + 1) * CHUNK, :]

    def prep_v(gi):
        for j, v in enumerate(conv_silu(2, gi)):
            hd = gi * HEAD_GROUP + j
            rhs_b[hd, :, 0:HEAD_DIM] = (v * beta[:, DN_HEADS + hd:DN_HEADS + hd + 1]).astype(BF16)

    def zgate_piece(blk):
        cols = slice(blk * GROUP_W, (blk + 1) * GROUP_W)
        zs_buf[:, cols] = _silu(_dot_nt(h, wzd_ref[cols, :]))

    def prep_pieces(gi):
        return [(prep_k, gi), (prep_q, gi), (prep_v, gi)]

    merge_gate_blocks = list(range(N_BRANCH * D_MODEL // GROUP_W))

    def emit_merge_gates(count):
        for _ in range(count):
            if merge_gate_blocks:
                blk = merge_gate_blocks.pop(0)
                cols = slice(blk * GROUP_W, (blk + 1) * GROUP_W)
                gates_ref[:, cols] = _sigmoid(_dot_nt(h, wg_ref[cols, :])).astype(BF16)

    def stage_score(items):
        for hd, i, sl in items:
            rows = slice(i * SPAN, (i + 1) * SPAN)
            diff = gc[rows, hd:hd + 1] - gct[i][hd:hd + 1, :]
            decay = jnp.where(tril, jnp.exp(jnp.where(tril, diff, 0.0)), 0.0)
            kq = _dot_nt(kq_lhs[hd, i], kn_b[hd, rows, :])
            a = jnp.where(stril, kq[top, :] * decay, 0.0)
            aqk_b[hd, i] = (kq[bot, :] * decay).astype(BF16)
            t0 = eye - a
            t_f[sl, i] = t0
            tp_b[sl, i, top, :] = t0.astype(BF16)
            tp_b[sl, i, bot, :] = a.astype(BF16)

    def stage_square(items):
        for hd, i, sl in items:
            a_b = tp_b[sl, i, bot, :]
            tp_b[sl, i, bot, :] = _dot(a_b, a_b).astype(BF16)

    def stage_combined(items):
        for hd, i, sl in items:
            r = _dot(tp_b[sl, i], tp_b[sl, i, bot, :])
            t_new = t_f[sl, i] + r[top, :]
            t_f[sl, i] = t_new
            tp_b[sl, i, top, :] = t_new.astype(BF16)
            tp_b[sl, i, bot, :] = r[bot, :].astype(BF16)

    def stage_final(items):
        for hd, i, sl in items:
            t_new = t_f[sl, i] + _dot(tp_b[sl, i, top, :], tp_b[sl, i, bot, :])
            tp_b[sl, i, top, :] = t_new.astype(BF16)

    def stage_uw(items):
        for hd, i, sl in items:
            rows = slice(i * SPAN, (i + 1) * SPAN)
            uw = _dot(tp_b[sl, i, top, :], rhs_b[hd, rows, :])
            u_f[hd, rows, :] = uw[:, :HEAD_DIM]
            w = uw[:, HEAD_DIM:].astype(BF16)
            for c in range(2):
                wq_b[hd, 2 * i + c, 0:CHUNK, :] = w[c * CHUNK:(c + 1) * CHUNK, :]

    n_combined = 0
    order = 2
    while 2 * order < CHUNK:
        n_combined += 1
        order *= 2
    stages = [stage_score, stage_square] + [stage_combined] * n_combined + [stage_final, stage_uw]

    n_slot = tp_b.shape[0]
    groups = [range(g0, g0 + HEAD_GROUP) for g0 in range(0, DN_HEADS, HEAD_GROUP)]
    for fn, arg in prep_pieces(0):
        fn(arg)
        emit_merge_gates(2)
    for gi, group in enumerate(groups):
        items = [(hd, i, hd % n_slot) for hd in group for i in range(n_span)]
        if gi + 1 < len(groups):
            fillers = prep_pieces(gi + 1)
        else:
            fillers = [(zgate_piece, blk) for blk in range(DN_WIDTH // GROUP_W)]
        every = len(stages) // len(fillers)
        for n, stage in enumerate(stages):
            stage(items)
            if n % every == 0 and n // every < len(fillers):
                fn, arg = fillers[n // every]
                fn(arg)

    zeros_chunk = jnp.zeros((CHUNK, HEAD_DIM), BF16)
    for sp in range(n_span):
        for c in range(2):
            rows = slice(c * CHUNK, (c + 1) * CHUNK)
            trows = slice(sp * SPAN + c * CHUNK, sp * SPAN + (c + 1) * CHUNK)
            last = sp * SPAN + (c + 1) * CHUNK - 1
            eg = e_buf[last:last + 1, :]
            for hd in heads:
                r = _dot(wq_b[hd, 2 * sp + c], state_ref[hd].astype(BF16))
                vn_b[hd, rows, :] = (u_f[hd, trows, :] - r[:CHUNK, :]).astype(BF16)
                oi_f[hd, rows, :] = r[CHUNK:, :]
            emit_merge_gates(1)
            for hd in heads:
                padded = [zeros_chunk, zeros_chunk]
                padded[c] = vn_b[hd, rows, :]
                upd = _dot(kdt_b[hd, sp], jnp.concatenate(padded, axis=0))
                state_ref[hd] = state_ref[hd] * eg[:, hd:hd + 1] + upd

        trows = slice(sp * SPAN, (sp + 1) * SPAN)
        for hd in heads:
            o = oi_f[hd] + _dot(aqk_b[hd, sp], vn_b[hd])
            cols = slice(hd * HEAD_DIM, (hd + 1) * HEAD_DIM)
            y = _rms(o, dnw_ref[...]) * zs_buf[trows, cols]
            yb_ref[trows, cols] = y.astype(BF16)
    emit_merge_gates(len(merge_gate_blocks))


def _delta(x, pre_norm_w, w_qkv, w_ab, w_zd, w_gate, conv_w, a_log, dt_bias, dn_norm_w):
    b, s, d = x.shape
    tm = TILE_DELTA
    n_span = tm // SPAN
    const2 = lambda i, j: (0, 0)
    tile = lambda i, j: (i, j, 0)
    return pl.pallas_call(
        _delta_kernel,
        grid=(b, s // tm),
        in_specs=[
            pl.BlockSpec((None, tm, d), tile),
            pl.BlockSpec((1, d), const2),
            pl.BlockSpec(w_qkv.shape, const2),
            pl.BlockSpec(w_ab.shape, const2),
            pl.BlockSpec(w_zd.shape, const2),
            pl.BlockSpec(w_gate.shape, const2),
            pl.BlockSpec(conv_w.shape, const2),
            pl.BlockSpec((1, LANES), const2),
            pl.BlockSpec((1, LANES), const2),
            pl.BlockSpec((1, HEAD_DIM), const2),
        ],
        out_specs=[
            pl.BlockSpec((None, tm, DN_WIDTH), tile),
            pl.BlockSpec((None, tm, N_BRANCH * D_MODEL), tile),
        ],
        out_shape=[
            jax.ShapeDtypeStruct((b, s, DN_WIDTH), BF16),
            jax.ShapeDtypeStruct((b, s, N_BRANCH * D_MODEL), BF16),
        ],
        scratch_shapes=[
            pltpu.VMEM((CONV_HALO, 3 * DN_WIDTH), F32),
            pltpu.VMEM((DN_HEADS, HEAD_DIM, HEAD_DIM), F32),
            pltpu.VMEM((tm, DN_WIDTH), F32),
            pltpu.VMEM((tm, LANES), F32),
            pltpu.VMEM((RAW_SLOTS, CONV_HALO + tm, GROUP_W), F32),
            pltpu.VMEM((DN_HEADS, n_span, 2 * SPAN, HEAD_DIM), BF16),
            pltpu.VMEM((DN_HEADS, tm, HEAD_DIM), BF16),
            pltpu.VMEM((DN_HEADS, tm, 2 * HEAD_DIM), BF16),
            pltpu.VMEM((DN_HEADS, tm // CHUNK, SPAN, HEAD_DIM), BF16),
            pltpu.VMEM((DN_HEADS, n_span, HEAD_DIM, SPAN), BF16),
            pltpu.VMEM((DN_HEADS, n_span, SPAN, SPAN), BF16),
            pltpu.VMEM((DN_HEADS, tm, HEAD_DIM), F32),
            pltpu.VMEM((2 * HEAD_GROUP, n_span, 2 * SPAN, SPAN), BF16),
            pltpu.VMEM((2 * HEAD_GROUP, n_span, SPAN, SPAN), F32),
            pltpu.VMEM((DN_HEADS, SPAN, HEAD_DIM), BF16),
            pltpu.VMEM((DN_HEADS, SPAN, HEAD_DIM), F32),
        ],
        compiler_params=pltpu.CompilerParams(
            dimension_semantics=("arbitrary", "arbitrary"), vmem_limit_bytes=VMEM_LIMIT_BYTES),
        name="delta_branch",
    )(x, pre_norm_w, w_qkv, w_ab, w_zd, w_gate, conv_w, a_log, dt_bias, dn_norm_w)


def _merge_kernel(x_ref, gates_ref, ya_ref, yb_ref, yc_ref, wpa_ref, wpb_ref, wpc_ref,
                  wout_ref, pw_ref, o_ref):
    y = None
    for br, (y_ref, w_ref) in enumerate(((ya_ref, wpa_ref), (yb_ref, wpb_ref), (yc_ref, wpc_ref))):
        gate = gates_ref[:, br * D_MODEL:(br + 1) * D_MODEL].astype(F32)
        term = gate * _dot(y_ref[...], w_ref[...])
        y = term if y is None else y + term
    out = _dot(y.astype(BF16), wout_ref[...])
    o_ref[...] = x_ref[...] + _rms(out, pw_ref[...])


def _merge(x2, gates, ya, yb, yc, w_pa, w_pb, w_pc, w_out, post_norm_w):
    n, d = x2.shape
    tm = TILE_MERGE
    const = lambda i: (0, 0)
    row = lambda i: (i, 0)
    return pl.pallas_call(
        _merge_kernel,
        grid=(n // tm,),
        in_specs=[
            pl.BlockSpec((tm, d), row),
            pl.BlockSpec((tm, N_BRANCH * d), row),
            pl.BlockSpec((tm, POOL_WIDTH), row),
            pl.BlockSpec((tm, DN_WIDTH), row),
            pl.BlockSpec((tm, MEM_WIDTH), row),
            pl.BlockSpec(w_pa.shape, const),
            pl.BlockSpec(w_pb.shape, const),
            pl.BlockSpec(w_pc.shape, const),
            pl.BlockSpec(w_out.shape, const),
            pl.BlockSpec((1, d), const),
        ],
        out_specs=pl.BlockSpec((tm, d), row),
        out_shape=jax.ShapeDtypeStruct((n, d), F32),
        compiler_params=pltpu.CompilerParams(
            dimension_semantics=("arbitrary",), vmem_limit_bytes=VMEM_LIMIT_BYTES),
        name="merge_out",
    )(x2, gates, ya, yb, yc, w_pa, w_pb, w_pc, w_out, post_norm_w)


def _layer(x, mem, pre_norm_w, mem_norm_w, w_in, conv_w, a_log, dt_bias, dn_norm_w, pool_mix_w,
           pool_scale, w_mem_kv, w_proj_pool, w_proj_delta, w_proj_mem, w_out, post_norm_w):
    b, s, d = x.shape
    assert d == D_MODEL and s % TILE_LOCAL == 0 and s % TILE_DELTA == 0
    assert (b * s) % TILE_MERGE == 0 and TILE_DELTA % SPAN == 0
    row = lambda v: v.reshape(1, -1).astype(F32)
    pad_lanes = lambda v: jnp.pad(v.astype(F32), (0, LANES - v.shape[0])).reshape(1, LANES)

    w_in_t = w_in.T
    w_loc = jnp.concatenate(
        [w_in_t[OFF_XA:OFF_Q], w_in_t[OFF_QM:OFF_GATE]], axis=0).astype(BF16)
    w_qkv = w_in_t[OFF_Q:OFF_A].astype(BF16)
    w_ab = jnp.pad(w_in_t[OFF_A:OFF_ZD], ((0, LANES - 2 * DN_HEADS), (0, 0))).astype(BF16)
    w_zd = w_in_t[OFF_ZD:OFF_QM].astype(BF16)
    w_gate = w_in_t[OFF_GATE:].astype(BF16)

    kt, v = _mem_kv(mem, row(mem_norm_w), w_mem_kv.astype(BF16))
    ya, yc = _local(x, row(pre_norm_w), w_loc, pool_mix_w.astype(BF16), row(pool_scale), kt, v)
    yb, gates = _delta(x, row(pre_norm_w), w_qkv, w_ab, w_zd, w_gate, conv_w.astype(F32),
                       pad_lanes(a_log), pad_lanes(dt_bias), row(dn_norm_w))
    out = _merge(x.reshape(b * s, d), gates.reshape(b * s, -1), ya.reshape(b * s, -1),
                 yb.reshape(b * s, -1), yc.reshape(b * s, -1), w_proj_pool.astype(BF16),
                 w_proj_delta.astype(BF16), w_proj_mem.astype(BF16), w_out.astype(BF16),
                 row(post_norm_w))
    return out.reshape(b, s, d)


def kernel(x, mem, pre_norm_w, mem_norm_w, w_in, conv_w, a_log, dt_bias, dn_norm_w, pool_mix_w,
           pool_scale, w_mem_kv, w_proj_pool, w_proj_delta, w_proj_mem, w_out, post_norm_w):
    for l in range(pre_norm_w.shape[0]):
        x = _layer(x, mem, pre_norm_w[l], mem_norm_w[l], w_in[l], conv_w[l], a_log[l], dt_bias[l],
                   dn_norm_w[l], pool_mix_w[l], pool_scale[l], w_mem_kv[l], w_proj_pool[l],
                   w_proj_delta[l], w_proj_mem[l], w_out[l], post_norm_w[l])
    return x
```

```python
import jax
import jax.numpy as jnp
from jax import lax
from jax.experimental import pallas as pl
from jax.experimental.pallas import tpu as pltpu

F32 = jnp.float32
BF16 = jnp.bfloat16

D_MODEL = 1024
HEAD_DIM = 128
DN_HEADS = D_MODEL // HEAD_DIM
DN_WIDTH = DN_HEADS * HEAD_DIM
POOL_WINDOWS = (2, 4, 8, 16)
POOL_GROUPS = len(POOL_WINDOWS)
POOL_WIDTH = D_MODEL // 2
POOL_GROUP_DIM = POOL_WIDTH // POOL_GROUPS
MEM_HEADS = 4
MEM_WIDTH = D_MODEL // 2
MEM_HEAD_DIM = MEM_WIDTH // MEM_HEADS
CONV_WIDTH = 4
CHUNK = 64
N_BRANCH = 3
EPS = 1e-6

OFF_XA = 0
OFF_ZA = OFF_XA + POOL_WIDTH
OFF_Q = OFF_ZA + POOL_WIDTH
OFF_K = OFF_Q + DN_WIDTH
OFF_V = OFF_K + DN_WIDTH
OFF_A = OFF_V + DN_WIDTH
OFF_B = OFF_A + DN_HEADS
OFF_ZD = OFF_B + DN_HEADS
OFF_QM = OFF_ZD + DN_WIDTH
OFF_ZM = OFF_QM + MEM_WIDTH
OFF_GATE = OFF_ZM + MEM_WIDTH

LANES = 128
SUBLANES = 8
SPAN = 2 * CHUNK
POOL_HALO = 16
CONV_HALO = SUBLANES
HEAD_GROUP = 2
GROUP_W = HEAD_GROUP * HEAD_DIM
RAW_SLOTS = 4
MERGE_GATE_W = 256
VMEM_LIMIT_BYTES = 56 * 1024 * 1024

TILE_LOCAL = 512
TILE_DELTA = 512
TILE_MERGE = 1024


def _rms(x, w):
    return x * lax.rsqrt(jnp.mean(x * x, axis=-1, keepdims=True) + EPS) * w


def _sigmoid(x):
    return 1.0 / (1.0 + jnp.exp(-x))


def _silu(x):
    return x * _sigmoid(x)


def _dot(a, b):
    return jnp.dot(a, b, preferred_element_type=F32)


def _dot_nt(a, b):
    return lax.dot_general(a, b, (((1,), (1,)), ((), ())), preferred_element_type=F32)


def _mem_kv_kernel(mem_ref, nw_ref, wkv_ref, kt_ref, v_ref):
    mn = _rms(mem_ref[...], nw_ref[...]).astype(BF16)
    kv = _dot(mn, wkv_ref[...])
    kt_ref[...] = kv[:, :MEM_WIDTH].T.astype(BF16)
    v_ref[...] = kv[:, MEM_WIDTH:].astype(BF16)


def _mem_kv(mem, mem_norm_w, w_kv):
    b, m, d = mem.shape
    return pl.pallas_call(
        _mem_kv_kernel,
        grid=(b,),
        in_specs=[
            pl.BlockSpec((None, m, d), lambda i: (i, 0, 0)),
            pl.BlockSpec((1, d), lambda i: (0, 0)),
            pl.BlockSpec((d, 2 * MEM_WIDTH), lambda i: (0, 0)),
        ],
        out_specs=[
            pl.BlockSpec((None, MEM_WIDTH, m), lambda i: (i, 0, 0)),
            pl.BlockSpec((None, m, MEM_WIDTH), lambda i: (i, 0, 0)),
        ],
        out_shape=[
            jax.ShapeDtypeStruct((b, MEM_WIDTH, m), BF16),
            jax.ShapeDtypeStruct((b, m, MEM_WIDTH), BF16),
        ],
        compiler_params=pltpu.CompilerParams(
            dimension_semantics=("arbitrary",), vmem_limit_bytes=VMEM_LIMIT_BYTES),
        name="mem_kv",
    )(mem, mem_norm_w, w_kv)


def _local_kernel(x_ref, nw_ref, w_ref, mix_ref, scale_ref, kt_ref, v_ref, ya_ref, yc_ref,
                  halo_ref):
    tm = x_ref.shape[0]
    s = pl.program_id(1)

    @pl.when(s == 0)
    def _():
        halo_ref[...] = jnp.zeros_like(halo_ref)

    h = _rms(x_ref[...], nw_ref[...]).astype(BF16)
    proj = _dot_nt(h, w_ref[...])
    xa = proj[:, 0:POOL_WIDTH]
    za = proj[:, POOL_WIDTH:2 * POOL_WIDTH]
    qm = proj[:, 2 * POOL_WIDTH:2 * POOL_WIDTH + MEM_WIDTH]
    zm = proj[:, 2 * POOL_WIDTH + MEM_WIDTH:]

    ext = jnp.concatenate([halo_ref[...], xa], axis=0)
    halo_ref[...] = xa[tm - POOL_HALO:, :]
    pos = (s * tm + 1 + lax.broadcasted_iota(jnp.int32, (tm, 1), 0)).astype(F32)
    mixed = []
    for g, window in enumerate(POOL_WINDOWS):
        cols = slice(g * POOL_GROUP_DIM, (g + 1) * POOL_GROUP_DIM)
        acc = ext[:, cols]
        width = 1
        while width < window:
            acc = acc + pltpu.roll(acc, width, 0)
            width *= 2
        mean = acc[POOL_HALO:, :] / jnp.minimum(pos, float(window))
        p = mean - xa[:, cols]
        mixed.append(_dot(p.astype(BF16), mix_ref[g]))
    m = jnp.concatenate(mixed, axis=1) * scale_ref[...]
    ya_ref[...] = (m * _silu(za)).astype(BF16)

    heads = []
    for hh in range(MEM_HEADS):
        cols = slice(hh * MEM_HEAD_DIM, (hh + 1) * MEM_HEAD_DIM)
        sc = _dot(qm[:, cols].astype(BF16), kt_ref[cols, :]) * (MEM_HEAD_DIM ** -0.5)
        e = jnp.exp(sc - jnp.max(sc, axis=-1, keepdims=True))
        o = _dot(e.astype(BF16), v_ref[:, cols])
        heads.append(o / jnp.sum(e, axis=-1, keepdims=True))
    yc_ref[...] = (jnp.concatenate(heads, axis=1) * _silu(zm)).astype(BF16)


def _local(x, pre_norm_w, w_loc, mix_w, pool_scale, kt, v):
    b, s, d = x.shape
    tm = TILE_LOCAL
    m = kt.shape[-1]
    const2 = lambda i, j: (0, 0)
    return pl.pallas_call(
        _local_kernel,
        grid=(b, s // tm),
        in_specs=[
            pl.BlockSpec((None, tm, d), lambda i, j: (i, j, 0)),
            pl.BlockSpec((1, d), const2),
            pl.BlockSpec(w_loc.shape, const2),
            pl.BlockSpec(mix_w.shape, lambda i, j: (0, 0, 0)),
            pl.BlockSpec((1, POOL_WIDTH), const2),
            pl.BlockSpec((None, MEM_WIDTH, m), lambda i, j: (i, 0, 0)),
            pl.BlockSpec((None, m, MEM_WIDTH), lambda i, j: (i, 0, 0)),
        ],
        out_specs=[
            pl.BlockSpec((None, tm, POOL_WIDTH), lambda i, j: (i, j, 0)),
            pl.BlockSpec((None, tm, MEM_WIDTH), lambda i, j: (i, j, 0)),
        ],
        out_shape=[
            jax.ShapeDtypeStruct((b, s, POOL_WIDTH), BF16),
            jax.ShapeDtypeStruct((b, s, MEM_WIDTH), BF16),
        ],
        scratch_shapes=[pltpu.VMEM((POOL_HALO, POOL_WIDTH), F32)],
        compiler_params=pltpu.CompilerParams(
            dimension_semantics=("arbitrary", "arbitrary"), vmem_limit_bytes=VMEM_LIMIT_BYTES),
        name="local_branches",
    )(x, pre_norm_w, w_loc, mix_w, pool_scale, kt, v)


def _delta_kernel(x_ref, nw_ref, wqkv_ref, wab_ref, wzd_ref, wg_ref, convw_ref, alog_ref, dtb_ref,
                  dnw_ref, yb_ref, gates_ref, halo_ref, state_ref, zs_buf, e_buf, raw_buf,
                  kq_lhs, kn_b, rhs_b, wq_b, kdt_b, aqk_b, u_f, tp_b, t_f, a_f, vn_b, oi_f):
    tm = x_ref.shape[0]
    n_span = tm // SPAN
    n_chunk = tm // CHUNK
    s = pl.program_id(1)
    heads = range(DN_HEADS)
    top = slice(0, SPAN)
    bot = slice(SPAN, 2 * SPAN)

    @pl.when(s == 0)
    def _():
        halo_ref[...] = jnp.zeros_like(halo_ref)
        state_ref[...] = jnp.zeros_like(state_ref)

    h = _rms(x_ref[...], nw_ref[...]).astype(BF16)

    ab = _dot_nt(h, wab_ref[...])
    sp_in = ab + dtb_ref[...]
    softplus = jnp.maximum(sp_in, 0.0) + jnp.log1p(jnp.exp(-jnp.abs(sp_in)))
    g = -jnp.exp(alog_ref[...]) * softplus
    beta = _sigmoid(ab)

    row_in_chunk = lax.broadcasted_iota(jnp.int32, (tm, LANES), 0) % CHUNK
    gc = g
    shift = 1
    while shift < CHUNK:
        gc = gc + jnp.where(row_in_chunk >= shift, pltpu.roll(gc, shift, 0), 0.0)
        shift *= 2
    e = jnp.exp(gc)
    e_buf[...] = e
    glast = jnp.concatenate(
        [jnp.broadcast_to(gc[(c + 1) * CHUNK - 1:(c + 1) * CHUNK, :], (CHUNK, LANES))
         for c in range(n_chunk)], axis=0)
    kdf = jnp.exp(glast - gc)

    ri = lax.broadcasted_iota(jnp.int32, (SPAN, SPAN), 0)
    ci = lax.broadcasted_iota(jnp.int32, (SPAN, SPAN), 1)
    same_chunk = (ri // CHUNK) == (ci // CHUNK)
    tril = same_chunk & (ri >= ci)
    stril = same_chunk & (ri > ci)
    eye = (ri == ci).astype(F32)
    gct = [gc[i * SPAN:(i + 1) * SPAN, :].T for i in range(n_span)]
    block_sizes = []
    size = 1
    while size < CHUNK:
        block_sizes.append(size)
        size *= 2
    sibling = {
        size: (((ri // (2 * size)) == (ci // (2 * size))) & ((ri // size) != (ci // size))).astype(F32)
        for size in block_sizes}

    n_raw = raw_buf.shape[0]

    def conv_silu(which, gi):
        blk = which * (DN_HEADS // HEAD_GROUP) + gi
        slot = blk % n_raw
        bcols = slice(blk * GROUP_W, (blk + 1) * GROUP_W)
        raw = _dot_nt(h, wqkv_ref[bcols, :])
        raw_buf[slot, 0:CONV_HALO, :] = halo_ref[:, bcols]
        raw_buf[slot, CONV_HALO:, :] = raw
        halo_ref[:, bcols] = raw[tm - CONV_HALO:, :]
        w = convw_ref[:, bcols]
        acc = raw * w[CONV_WIDTH - 1:CONV_WIDTH, :]
        for back in range(1, CONV_WIDTH):
            tap = CONV_WIDTH - 1 - back
            start = CONV_HALO - back
            acc = acc + raw_buf[slot, start:start + tm, :] * w[tap:tap + 1, :]
        out = _silu(acc)
        return [out[:, j * HEAD_DIM:(j + 1) * HEAD_DIM] for j in range(HEAD_GROUP)]

    def prep_k(gi):
        for j, k in enumerate(conv_silu(1, gi)):
            hd = gi * HEAD_GROUP + j
            kn = k * lax.rsqrt(jnp.sum(k * k, axis=-1, keepdims=True) + EPS)
            kb = kn * beta[:, DN_HEADS + hd:DN_HEADS + hd + 1]
            kb_b = kb.astype(BF16)
            kd = kn * kdf[:, hd:hd + 1]
            kn_b[hd] = kn.astype(BF16)
            rhs_b[hd, :, HEAD_DIM:] = (kb * e[:, hd:hd + 1]).astype(BF16)
            for i in range(n_span):
                rows = slice(i * SPAN, (i + 1) * SPAN)
                kq_lhs[hd, i, top, :] = kb_b[rows, :]
                kdt_b[hd, i] = kd[rows, :].T.astype(BF16)

    def prep_q(gi):
        for j, q in enumerate(conv_silu(0, gi)):
            hd = gi * HEAD_GROUP + j
            qn = q * lax.rsqrt(jnp.sum(q * q, axis=-1, keepdims=True) + EPS) * (HEAD_DIM ** -0.5)
            qn_b = qn.astype(BF16)
            qg = (qn * e[:, hd:hd + 1]).astype(BF16)
            for i in range(n_span):
                kq_lhs[hd, i, bot, :] = qn_b[i * SPAN:(i + 1) * SPAN, :]
            for c in range(n_chunk):
                wq_b[hd, c, CHUNK:, :] = qg[c * CHUNK:(c + 1) * CHUNK, :]

    def prep_v(gi):
        for j, v in enumerate(conv_silu(2, gi)):
            hd = gi * HEAD_GROUP + j
            rhs_b[hd, :, 0:HEAD_DIM] = (v * beta[:, DN_HEADS + hd:DN_HEADS + hd + 1]).astype(BF16)

    def zgate_piece(blk):
        cols = slice(blk * GROUP_W, (blk + 1) * GROUP_W)
        zs_buf[:, cols] = _silu(_dot_nt(h, wzd_ref[cols, :]))

    def prep_pieces(gi):
        return [(prep_k, gi), (prep_q, gi), (prep_v, gi)]

    merge_gate_blocks = list(range(N_BRANCH * D_MODEL // MERGE_GATE_W))

    def emit_merge_gates(count):
        for _ in range(count):
            if merge_gate_blocks:
                blk = merge_gate_blocks.pop(0)
                cols = slice(blk * MERGE_GATE_W, (blk + 1) * MERGE_GATE_W)
                gates_ref[:, cols] = _sigmoid(_dot_nt(h, wg_ref[cols, :])).astype(BF16)

    def stage_score(items):
        for hd, i, sl in items:
            rows = slice(i * SPAN, (i + 1) * SPAN)
            diff = gc[rows, hd:hd + 1] - gct[i][hd:hd + 1, :]
            decay = jnp.where(tril, jnp.exp(jnp.where(tril, diff, 0.0)), 0.0)
            kq = _dot_nt(kq_lhs[hd, i], kn_b[hd, rows, :])
            a = jnp.where(stril, kq[top, :] * decay, 0.0)
            aqk_b[hd, i] = (kq[bot, :] * decay).astype(BF16)
            a_f[sl, i] = a
            t1 = eye - a * sibling[1]
            t_f[sl, i] = t1
            tp_b[sl, i, top, :] = t1.astype(BF16)

    def stage_left(size):
        def run(items):
            for hd, i, sl in items:
                a_s = (a_f[sl, i] * sibling[size]).astype(BF16)
                tp_b[sl, i, bot, :] = _dot(tp_b[sl, i, top, :], a_s).astype(BF16)
        return run

    def stage_right(items):
        for hd, i, sl in items:
            t_new = t_f[sl, i] - _dot(tp_b[sl, i, bot, :], tp_b[sl, i, top, :])
            t_f[sl, i] = t_new
            tp_b[sl, i, top, :] = t_new.astype(BF16)

    def stage_uw(items):
        for hd, i, sl in items:
            rows = slice(i * SPAN, (i + 1) * SPAN)
            uw = _dot(tp_b[sl, i, top, :], rhs_b[hd, rows, :])
            u_f[hd, rows, :] = uw[:, :HEAD_DIM]
            w = uw[:, HEAD_DIM:].astype(BF16)
            for c in range(2):
                wq_b[hd, 2 * i + c, 0:CHUNK, :] = w[c * CHUNK:(c + 1) * CHUNK, :]

    stages = [stage_score]
    for size in block_sizes[1:]:
        stages += [stage_left(size), stage_right]
    stages.append(stage_uw)

    n_slot = tp_b.shape[0]
    groups = [range(g0, g0 + HEAD_GROUP) for g0 in range(0, DN_HEADS, HEAD_GROUP)]
    for fn, arg in prep_pieces(0):
        fn(arg)
        emit_merge_gates(2)
    for gi, group in enumerate(groups):
        items = [(hd, i, hd % n_slot) for hd in group for i in range(n_span)]
        if gi + 1 < len(groups):
            fillers = prep_pieces(gi + 1)
        else:
            fillers = [(zgate_piece, blk) for blk in range(DN_WIDTH // GROUP_W)]
        every = len(stages) // len(fillers)
        for n, stage in enumerate(stages):
            stage(items)
            if n % every == 0 and n // every < len(fillers):
                fn, arg = fillers[n // every]
                fn(arg)

    zeros_chunk = jnp.zeros((CHUNK, HEAD_DIM), BF16)
    for sp in range(n_span):
        for c in range(2):
            rows = slice(c * CHUNK, (c + 1) * CHUNK)
            trows = slice(sp * SPAN + c * CHUNK, sp * SPAN + (c + 1) * CHUNK)
            last = sp * SPAN + (c + 1) * CHUNK - 1
            eg = e_buf[last:last + 1, :]
            for hd in heads:
                r = _dot(wq_b[hd, 2 * sp + c], state_ref[hd].astype(BF16))
                vn_b[hd, rows, :] = (u_f[hd, trows, :] - r[:CHUNK, :]).astype(BF16)
                oi_f[hd, rows, :] = r[CHUNK:, :]
            emit_merge_gates(1)
            for hd in heads:
                padded = [zeros_chunk, zeros_chunk]
                padded[c] = vn_b[hd, rows, :]
                upd = _dot(kdt_b[hd, sp], jnp.concatenate(padded, axis=0))
                state_ref[hd] = state_ref[hd] * eg[:, hd:hd + 1] + upd

        trows = slice(sp * SPAN, (sp + 1) * SPAN)
        for hd in heads:
            o = oi_f[hd] + _dot(aqk_b[hd, sp], vn_b[hd])
            cols = slice(hd * HEAD_DIM, (hd + 1) * HEAD_DIM)
            y = _rms(o, dnw_ref[...]) * zs_buf[trows, cols]
            yb_ref[trows, cols] = y.astype(BF16)
    emit_merge_gates(len(merge_gate_blocks))


def _delta(x, pre_norm_w, w_qkv, w_ab, w_zd, w_gate, conv_w, a_log, dt_bias, dn_norm_w):
    b, s, d = x.shape
    tm = TILE_DELTA
    n_span = tm // SPAN
    const2 = lambda i, j: (0, 0)
    tile = lambda i, j: (i, j, 0)
    return pl.pallas_call(
        _delta_kernel,
        grid=(b, s // tm),
        in_specs=[
            pl.BlockSpec((None, tm, d), tile),
            pl.BlockSpec((1, d), const2),
            pl.BlockSpec(w_qkv.shape, const2),
            pl.BlockSpec(w_ab.shape, const2),
            pl.BlockSpec(w_zd.shape, const2),
            pl.BlockSpec(w_gate.shape, const2),
            pl.BlockSpec(conv_w.shape, const2),
            pl.BlockSpec((1, LANES), const2),
            pl.BlockSpec((1, LANES), const2),
            pl.BlockSpec((1, HEAD_DIM), const2),
        ],
        out_specs=[
            pl.BlockSpec((None, tm, DN_WIDTH), tile),
            pl.BlockSpec((None, tm, N_BRANCH * D_MODEL), tile),
        ],
        out_shape=[
            jax.ShapeDtypeStruct((b, s, DN_WIDTH), BF16),
            jax.ShapeDtypeStruct((b, s, N_BRANCH * D_MODEL), BF16),
        ],
        scratch_shapes=[
            pltpu.VMEM((CONV_HALO, 3 * DN_WIDTH), F32),
            pltpu.VMEM((DN_HEADS, HEAD_DIM, HEAD_DIM), F32),
            pltpu.VMEM((tm, DN_WIDTH), F32),
            pltpu.VMEM((tm, LANES), F32),
            pltpu.VMEM((RAW_SLOTS, CONV_HALO + tm, GROUP_W), F32),
            pltpu.VMEM((DN_HEADS, n_span, 2 * SPAN, HEAD_DIM), BF16),
            pltpu.VMEM((DN_HEADS, tm, HEAD_DIM), BF16),
            pltpu.VMEM((DN_HEADS, tm, 2 * HEAD_DIM), BF16),
            pltpu.VMEM((DN_HEADS, tm // CHUNK, SPAN, HEAD_DIM), BF16),
            pltpu.VMEM((DN_HEADS, n_span, HEAD_DIM, SPAN), BF16),
            pltpu.VMEM((DN_HEADS, n_span, SPAN, SPAN), BF16),
            pltpu.VMEM((DN_HEADS, tm, HEAD_DIM), F32),
            pltpu.VMEM((2 * HEAD_GROUP, n_span, 2 * SPAN, SPAN), BF16),
            pltpu.VMEM((2 * HEAD_GROUP, n_span, SPAN, SPAN), F32),
            pltpu.VMEM((2 * HEAD_GROUP, n_span, SPAN, SPAN), F32),
            pltpu.VMEM((DN_HEADS, SPAN, HEAD_DIM), BF16),
            pltpu.VMEM((DN_HEADS, SPAN, HEAD_DIM), F32),
        ],
        compiler_params=pltpu.CompilerParams(
            dimension_semantics=("arbitrary", "arbitrary"), vmem_limit_bytes=VMEM_LIMIT_BYTES),
        name="delta_branch",
    )(x, pre_norm_w, w_qkv, w_ab, w_zd, w_gate, conv_w, a_log, dt_bias, dn_norm_w)


def _merge_kernel(x_ref, gates_ref, ya_ref, yb_ref, yc_ref, wpa_ref, wpb_ref, wpc_ref,
                  wout_ref, pw_ref, o_ref):
    y = None
    for br, (y_ref, w_ref) in enumerate(((ya_ref, wpa_ref), (yb_ref, wpb_ref), (yc_ref, wpc_ref))):
        gate = gates_ref[:, br * D_MODEL:(br + 1) * D_MODEL].astype(F32)
        term = gate * _dot(y_ref[...], w_ref[...])
        y = term if y is None else y + term
    out = _dot(y.astype(BF16), wout_ref[...])
    o_ref[...] = x_ref[...] + _rms(out, pw_ref[...])


def _merge(x2, gates, ya, yb, yc, w_pa, w_pb, w_pc, w_out, post_norm_w):
    n, d = x2.shape
    tm = TILE_MERGE
    const = lambda i: (0, 0)
    row = lambda i: (i, 0)
    return pl.pallas_call(
        _merge_kernel,
        grid=(n // tm,),
        in_specs=[
            pl.BlockSpec((tm, d), row),
            pl.BlockSpec((tm, N_BRANCH * d), row),
            pl.BlockSpec((tm, POOL_WIDTH), row),
            pl.BlockSpec((tm, DN_WIDTH), row),
            pl.BlockSpec((tm, MEM_WIDTH), row),
            pl.BlockSpec(w_pa.shape, const),
            pl.BlockSpec(w_pb.shape, const),
            pl.BlockSpec(w_pc.shape, const),
            pl.BlockSpec(w_out.shape, const),
            pl.BlockSpec((1, d), const),
        ],
        out_specs=pl.BlockSpec((tm, d), row),
        out_shape=jax.ShapeDtypeStruct((n, d), F32),
        compiler_params=pltpu.CompilerParams(
            dimension_semantics=("arbitrary",), vmem_limit_bytes=VMEM_LIMIT_BYTES),
        name="merge_out",
    )(x2, gates, ya, yb, yc, w_pa, w_pb, w_pc, w_out, post_norm_w)


def _layer(x, mem, pre_norm_w, mem_norm_w, w_in, conv_w, a_log, dt_bias, dn_norm_w, pool_mix_w,
           pool_scale, w_mem_kv, w_proj_pool, w_proj_delta, w_proj_mem, w_out, post_norm_w):
    b, s, d = x.shape
    assert d == D_MODEL and s % TILE_LOCAL == 0 and s % TILE_DELTA == 0
    assert (b * s) % TILE_MERGE == 0 and TILE_DELTA % SPAN == 0
    row = lambda v: v.reshape(1, -1).astype(F32)
    pad_lanes = lambda v: jnp.pad(v.astype(F32), (0, LANES - v.shape[0])).reshape(1, LANES)

    w_in_t = w_in.T
    w_loc = jnp.concatenate(
        [w_in_t[OFF_XA:OFF_Q], w_in_t[OFF_QM:OFF_GATE]], axis=0).astype(BF16)
    w_qkv = w_in_t[OFF_Q:OFF_A].astype(BF16)
    w_ab = jnp.pad(w_in_t[OFF_A:OFF_ZD], ((0, LANES - 2 * DN_HEADS), (0, 0))).astype(BF16)
    w_zd = w_in_t[OFF_ZD:OFF_QM].astype(BF16)
    w_gate = w_in_t[OFF_GATE:].astype(BF16)

    kt, v = _mem_kv(mem, row(mem_norm_w), w_mem_kv.astype(BF16))
    ya, yc = _local(x, row(pre_norm_w), w_loc, pool_mix_w.astype(BF16), row(pool_scale), kt, v)
    yb, gates = _delta(x, row(pre_norm_w), w_qkv, w_ab, w_zd, w_gate, conv_w.astype(F32),
                       pad_lanes(a_log), pad_lanes(dt_bias), row(dn_norm_w))
    out = _merge(x.reshape(b * s, d), gates.reshape(b * s, -1), ya.reshape(b * s, -1),
                 yb.reshape(b * s, -1), yc.reshape(b * s, -1), w_proj_pool.astype(BF16),
                 w_proj_delta.astype(BF16), w_proj_mem.astype(BF16), w_out.astype(BF16),
                 row(post_norm_w))
    return out.reshape(b, s, d)


def kernel(x, mem, pre_norm_w, mem_norm_w, w_in, conv_w, a_log, dt_bias, dn_norm_w, pool_mix_w,
           pool_scale, w_mem_kv, w_proj_pool, w_proj_delta, w_proj_mem, w_out, post_norm_w):
    for l in range(pre_norm_w.shape[0]):
        x = _layer(x, mem, pre_norm_w[l], mem_norm_w[l], w_in[l], conv_w[l], a_log[l], dt_bias[l],
                   dn_norm_w[l], pool_mix_w[l], pool_scale[l], w_mem_kv[l], w_proj_pool[l],
                   w_proj_delta[l], w_proj_mem[l], w_out[l], post_norm_w[l])
    return x
```

```python
import jax
import jax.numpy as jnp
from jax import lax
from jax.experimental import pallas as pl
from jax.experimental.pallas import tpu as pltpu

F32 = jnp.float32
BF16 = jnp.bfloat16

D_MODEL = 1024
HEAD_DIM = 128
DN_HEADS = D_MODEL // HEAD_DIM
DN_WIDTH = DN_HEADS * HEAD_DIM
POOL_WINDOWS = (2, 4, 8, 16)
POOL_GROUPS = len(POOL_WINDOWS)
POOL_WIDTH = D_MODEL // 2
POOL_GROUP_DIM = POOL_WIDTH // POOL_GROUPS
MEM_HEADS = 4
MEM_WIDTH = D_MODEL // 2
MEM_HEAD_DIM = MEM_WIDTH // MEM_HEADS
CONV_WIDTH = 4
CHUNK = 64
N_BRANCH = 3
EPS = 1e-6

OFF_XA = 0
OFF_ZA = OFF_XA + POOL_WIDTH
OFF_Q = OFF_ZA + POOL_WIDTH
OFF_K = OFF_Q + DN_WIDTH
OFF_V = OFF_K + DN_WIDTH
OFF_A = OFF_V + DN_WIDTH
OFF_B = OFF_A + DN_HEADS
OFF_ZD = OFF_B + DN_HEADS
OFF_QM = OFF_ZD + DN_WIDTH
OFF_ZM = OFF_QM + MEM_WIDTH
OFF_GATE = OFF_ZM + MEM_WIDTH

LANES = 128
SUBLANES = 8
SPAN = 2 * CHUNK
POOL_HALO = 16
CONV_HALO = SUBLANES
HEAD_GROUP = 2
GROUP_W = HEAD_GROUP * HEAD_DIM
RAW_SLOTS = 4
MERGE_GATE_W = 256
VMEM_LIMIT_BYTES = 56 * 1024 * 1024

TILE_LOCAL = 512
TILE_DELTA = 512
TILE_MERGE = 1024


def _rms(x, w):
    return x * lax.rsqrt(jnp.mean(x * x, axis=-1, keepdims=True) + EPS) * w


def _sigmoid(x):
    return 1.0 / (1.0 + jnp.exp(-x))


def _silu(x):
    return x * _sigmoid(x)


def _dot(a, b):
    return jnp.dot(a, b, preferred_element_type=F32)


def _dot_nt(a, b):
    return lax.dot_general(a, b, (((1,), (1,)), ((), ())), preferred_element_type=F32)


def _mem_kv_kernel(mem_ref, nw_ref, wkv_ref, kt_ref, v_ref):
    mn = _rms(mem_ref[...], nw_ref[...]).astype(BF16)
    kv = _dot(mn, wkv_ref[...])
    kt_ref[...] = kv[:, :MEM_WIDTH].T.astype(BF16)
    v_ref[...] = kv[:, MEM_WIDTH:].astype(BF16)


def _mem_kv(mem, mem_norm_w, w_kv):
    b, m, d = mem.shape
    return pl.pallas_call(
        _mem_kv_kernel,
        grid=(b,),
        in_specs=[
            pl.BlockSpec((None, m, d), lambda i: (i, 0, 0)),
            pl.BlockSpec((1, d), lambda i: (0, 0)),
            pl.BlockSpec((d, 2 * MEM_WIDTH), lambda i: (0, 0)),
        ],
        out_specs=[
            pl.BlockSpec((None, MEM_WIDTH, m), lambda i: (i, 0, 0)),
            pl.BlockSpec((None, m, MEM_WIDTH), lambda i: (i, 0, 0)),
        ],
        out_shape=[
            jax.ShapeDtypeStruct((b, MEM_WIDTH, m), BF16),
            jax.ShapeDtypeStruct((b, m, MEM_WIDTH), BF16),
        ],
        compiler_params=pltpu.CompilerParams(
            dimension_semantics=("arbitrary",), vmem_limit_bytes=VMEM_LIMIT_BYTES),
        name="mem_kv",
    )(mem, mem_norm_w, w_kv)


def _local_kernel(x_ref, nw_ref, wpool_ref, wmem_ref, mix_ref, scale_ref, kt_ref, v_ref, ya_ref,
                  yc_ref, halo_ref):
    tm = x_ref.shape[0]
    s = pl.program_id(1)

    @pl.when(s == 0)
    def _():
        halo_ref[...] = jnp.zeros_like(halo_ref)

    h = _rms(x_ref[...], nw_ref[...]).astype(BF16)
    proj_pool = _dot_nt(h, wpool_ref[...])
    proj_mem = _dot_nt(h, wmem_ref[...])
    xa = proj_pool[:, 0:POOL_WIDTH]
    za = proj_pool[:, POOL_WIDTH:]
    qm = proj_mem[:, 0:MEM_WIDTH]
    zm = proj_mem[:, MEM_WIDTH:]

    ext = jnp.concatenate([halo_ref[...], xa], axis=0)
    halo_ref[...] = xa[tm - POOL_HALO:, :]
    pos = (s * tm + 1 + lax.broadcasted_iota(jnp.int32, (tm, 1), 0)).astype(F32)
    mixed = []
    for g, window in enumerate(POOL_WINDOWS):
        cols = slice(g * POOL_GROUP_DIM, (g + 1) * POOL_GROUP_DIM)
        acc = ext[:, cols]
        width = 1
        while width < window:
            acc = acc + pltpu.roll(acc, width, 0)
            width *= 2
        mean = acc[POOL_HALO:, :] / jnp.minimum(pos, float(window))
        p = mean - xa[:, cols]
        mixed.append(_dot(p.astype(BF16), mix_ref[g]))
    m = jnp.concatenate(mixed, axis=1) * scale_ref[...]
    ya_ref[...] = (m * _silu(za)).astype(BF16)

    heads = []
    for hh in range(MEM_HEADS):
        cols = slice(hh * MEM_HEAD_DIM, (hh + 1) * MEM_HEAD_DIM)
        sc = _dot(qm[:, cols].astype(BF16), kt_ref[cols, :]) * (MEM_HEAD_DIM ** -0.5)
        e = jnp.exp(sc - jnp.max(sc, axis=-1, keepdims=True))
        o = _dot(e.astype(BF16), v_ref[:, cols])
        heads.append(o / jnp.sum(e, axis=-1, keepdims=True))
    yc_ref[...] = (jnp.concatenate(heads, axis=1) * _silu(zm)).astype(BF16)


def _w_in_rows(start, size):
    return pl.BlockSpec((pl.Element(size), pl.Element(D_MODEL)), lambda *_: (start, 0))


def _local(x, pre_norm_w, w_in_t, mix_w, pool_scale, kt, v):
    b, s, d = x.shape
    tm = TILE_LOCAL
    m = kt.shape[-1]
    const2 = lambda i, j: (0, 0)
    return pl.pallas_call(
        _local_kernel,
        grid=(b, s // tm),
        in_specs=[
            pl.BlockSpec((None, tm, d), lambda i, j: (i, j, 0)),
            pl.BlockSpec((1, d), const2),
            _w_in_rows(OFF_XA, OFF_Q - OFF_XA),
            _w_in_rows(OFF_QM, OFF_GATE - OFF_QM),
            pl.BlockSpec(mix_w.shape, lambda i, j: (0, 0, 0)),
            pl.BlockSpec((1, POOL_WIDTH), const2),
            pl.BlockSpec((None, MEM_WIDTH, m), lambda i, j: (i, 0, 0)),
            pl.BlockSpec((None, m, MEM_WIDTH), lambda i, j: (i, 0, 0)),
        ],
        out_specs=[
            pl.BlockSpec((None, tm, POOL_WIDTH), lambda i, j: (i, j, 0)),
            pl.BlockSpec((None, tm, MEM_WIDTH), lambda i, j: (i, j, 0)),
        ],
        out_shape=[
            jax.ShapeDtypeStruct((b, s, POOL_WIDTH), BF16),
            jax.ShapeDtypeStruct((b, s, MEM_WIDTH), BF16),
        ],
        scratch_shapes=[pltpu.VMEM((POOL_HALO, POOL_WIDTH), F32)],
        compiler_params=pltpu.CompilerParams(
            dimension_semantics=("arbitrary", "arbitrary"), vmem_limit_bytes=VMEM_LIMIT_BYTES),
        name="local_branches",
    )(x, pre_norm_w, w_in_t, w_in_t, mix_w, pool_scale, kt, v)


def _delta_kernel(x_ref, nw_ref, wqkv_ref, wab_ref, wzd_ref, wg_ref, convw_ref, alog_ref, dtb_ref,
                  dnw_ref, yb_ref, gates_ref, halo_ref, state_ref, zs_buf, e_buf, raw_buf,
                  kq_lhs, kn_b, rhs_b, wq_b, kdt_b, aqk_b, u_f, tp_b, t_f, a_f, vn_b, oi_f):
    tm = x_ref.shape[0]
    n_span = tm // SPAN
    n_chunk = tm // CHUNK
    s = pl.program_id(1)
    heads = range(DN_HEADS)
    top = slice(0, SPAN)
    bot = slice(SPAN, 2 * SPAN)

    @pl.when(s == 0)
    def _():
        halo_ref[...] = jnp.zeros_like(halo_ref)
        state_ref[...] = jnp.zeros_like(state_ref)

    h = _rms(x_ref[...], nw_ref[...]).astype(BF16)

    ab = _dot_nt(h, wab_ref[...])
    sp_in = ab + dtb_ref[...]
    softplus = jnp.maximum(sp_in, 0.0) + jnp.log1p(jnp.exp(-jnp.abs(sp_in)))
    g = -jnp.exp(alog_ref[...]) * softplus
    beta = _sigmoid(ab)

    row_in_chunk = lax.broadcasted_iota(jnp.int32, (tm, LANES), 0) % CHUNK
    gc = g
    shift = 1
    while shift < CHUNK:
        gc = gc + jnp.where(row_in_chunk >= shift, pltpu.roll(gc, shift, 0), 0.0)
        shift *= 2
    e = jnp.exp(gc)
    e_buf[...] = e
    glast = jnp.concatenate(
        [jnp.broadcast_to(gc[(c + 1) * CHUNK - 1:(c + 1) * CHUNK, :], (CHUNK, LANES))
         for c in range(n_chunk)], axis=0)
    kdf = jnp.exp(glast - gc)

    ri = lax.broadcasted_iota(jnp.int32, (SPAN, SPAN), 0)
    ci = lax.broadcasted_iota(jnp.int32, (SPAN, SPAN), 1)
    same_chunk = (ri // CHUNK) == (ci // CHUNK)
    tril = same_chunk & (ri >= ci)
    stril = same_chunk & (ri > ci)
    eye = (ri == ci).astype(F32)
    gct = [gc[i * SPAN:(i + 1) * SPAN, :].T for i in range(n_span)]
    block_sizes = []
    size = 1
    while size < CHUNK:
        block_sizes.append(size)
        size *= 2
    sibling = {
        size: (((ri // (2 * size)) == (ci // (2 * size))) & ((ri // size) != (ci // size))).astype(F32)
        for size in block_sizes}

    n_raw = raw_buf.shape[0]

    def conv_silu(which, gi):
        blk = which * (DN_HEADS // HEAD_GROUP) + gi
        slot = blk % n_raw
        bcols = slice(blk * GROUP_W, (blk + 1) * GROUP_W)
        raw = _dot_nt(h, wqkv_ref[bcols, :])
        raw_buf[slot, 0:CONV_HALO, :] = halo_ref[:, bcols]
        raw_buf[slot, CONV_HALO:, :] = raw
        halo_ref[:, bcols] = raw[tm - CONV_HALO:, :]
        w = convw_ref[:, bcols]
        acc = raw * w[CONV_WIDTH - 1:CONV_WIDTH, :]
        for back in range(1, CONV_WIDTH):
            tap = CONV_WIDTH - 1 - back
            start = CONV_HALO - back
            acc = acc + raw_buf[slot, start:start + tm, :] * w[tap:tap + 1, :]
        out = _silu(acc)
        return [out[:, j * HEAD_DIM:(j + 1) * HEAD_DIM] for j in range(HEAD_GROUP)]

    def prep_k(gi):
        for j, k in enumerate(conv_silu(1, gi)):
            hd = gi * HEAD_GROUP + j
            kn = k * lax.rsqrt(jnp.sum(k * k, axis=-1, keepdims=True) + EPS)
            kb = kn * beta[:, DN_HEADS + hd:DN_HEADS + hd + 1]
            kb_b = kb.astype(BF16)
            kd = kn * kdf[:, hd:hd + 1]
            kn_b[hd] = kn.astype(BF16)
            rhs_b[hd, :, HEAD_DIM:] = (kb * e[:, hd:hd + 1]).astype(BF16)
            for i in range(n_span):
                rows = slice(i * SPAN, (i + 1) * SPAN)
                kq_lhs[hd, i, top, :] = kb_b[rows, :]
                kdt_b[hd, i] = kd[rows, :].T.astype(BF16)

    def prep_q(gi):
        for j, q in enumerate(conv_silu(0, gi)):
            hd = gi * HEAD_GROUP + j
            qn = q * lax.rsqrt(jnp.sum(q * q, axis=-1, keepdims=True) + EPS) * (HEAD_DIM ** -0.5)
            qn_b = qn.astype(BF16)
            qg = (qn * e[:, hd:hd + 1]).astype(BF16)
            for i in range(n_span):
                kq_lhs[hd, i, bot, :] = qn_b[i * SPAN:(i + 1) * SPAN, :]
            for c in range(n_chunk):
                wq_b[hd, c, CHUNK:, :] = qg[c * CHUNK:(c + 1) * CHUNK, :]

    def prep_v(gi):
        for j, v in enumerate(conv_silu(2, gi)):
            hd = gi * HEAD_GROUP + j
            rhs_b[hd, :, 0:HEAD_DIM] = (v * beta[:, DN_HEADS + hd:DN_HEADS + hd + 1]).astype(BF16)

    def zgate_piece(blk):
        cols = slice(blk * GROUP_W, (blk + 1) * GROUP_W)
        zs_buf[:, cols] = _silu(_dot_nt(h, wzd_ref[cols, :]))

    def prep_pieces(gi):
        return [(prep_k, gi), (prep_q, gi), (prep_v, gi)]

    merge_gate_blocks = list(range(N_BRANCH * D_MODEL // MERGE_GATE_W))

    def emit_merge_gates(count):
        for _ in range(count):
            if merge_gate_blocks:
                blk = merge_gate_blocks.pop(0)
                cols = slice(blk * MERGE_GATE_W, (blk + 1) * MERGE_GATE_W)
                gates_ref[:, cols] = _sigmoid(_dot_nt(h, wg_ref[cols, :])).astype(BF16)

    def stage_score(items):
        for hd, i, sl in items:
            rows = slice(i * SPAN, (i + 1) * SPAN)
            diff = gc[rows, hd:hd + 1] - gct[i][hd:hd + 1, :]
            decay = jnp.where(tril, jnp.exp(jnp.where(tril, diff, 0.0)), 0.0)
            kq = _dot_nt(kq_lhs[hd, i], kn_b[hd, rows, :])
            a = jnp.where(stril, kq[top, :] * decay, 0.0)
            aqk_b[hd, i] = (kq[bot, :] * decay).astype(BF16)
            a_f[sl, i] = a
            t1 = eye - a * sibling[1]
            t_f[sl, i] = t1
            tp_b[sl, i, top, :] = t1.astype(BF16)

    def stage_left(size):
        def run(items):
            for hd, i, sl in items:
                a_s = (a_f[sl, i] * sibling[size]).astype(BF16)
                tp_b[sl, i, bot, :] = _dot(tp_b[sl, i, top, :], a_s).astype(BF16)
        return run

    def stage_right(items):
        for hd, i, sl in items:
            t_new = t_f[sl, i] - _dot(tp_b[sl, i, bot, :], tp_b[sl, i, top, :])
            t_f[sl, i] = t_new
            tp_b[sl, i, top, :] = t_new.astype(BF16)

    def stage_uw(items):
        for hd, i, sl in items:
            rows = slice(i * SPAN, (i + 1) * SPAN)
            uw = _dot(tp_b[sl, i, top, :], rhs_b[hd, rows, :])
            u_f[hd, rows, :] = uw[:, :HEAD_DIM]
            w = uw[:, HEAD_DIM:].astype(BF16)
            for c in range(2):
                wq_b[hd, 2 * i + c, 0:CHUNK, :] = w[c * CHUNK:(c + 1) * CHUNK, :]

    stages = [stage_score]
    for size in block_sizes[1:]:
        stages += [stage_left(size), stage_right]
    stages.append(stage_uw)

    n_slot = tp_b.shape[0]
    groups = [range(g0, g0 + HEAD_GROUP) for g0 in range(0, DN_HEADS, HEAD_GROUP)]
    for fn, arg in prep_pieces(0):
        fn(arg)
        emit_merge_gates(2)
    for gi, group in enumerate(groups):
        items = [(hd, i, hd % n_slot) for hd in group for i in range(n_span)]
        if gi + 1 < len(groups):
            fillers = prep_pieces(gi + 1)
        else:
            fillers = [(zgate_piece, blk) for blk in range(DN_WIDTH // GROUP_W)]
        every = len(stages) // len(fillers)
        for n, stage in enumerate(stages):
            stage(items)
            if n % every == 0 and n // every < len(fillers):
                fn, arg = fillers[n // every]
                fn(arg)

    zeros_chunk = jnp.zeros((CHUNK, HEAD_DIM), BF16)
    for sp in range(n_span):
        for c in range(2):
            rows = slice(c * CHUNK, (c + 1) * CHUNK)
            trows = slice(sp * SPAN + c * CHUNK, sp * SPAN + (c + 1) * CHUNK)
            last = sp * SPAN + (c + 1) * CHUNK - 1
            eg = e_buf[last:last + 1, :]
            for hd in heads:
                r = _dot(wq_b[hd, 2 * sp + c], state_ref[hd].astype(BF16))
                vn_b[hd, rows, :] = (u_f[hd, trows, :] - r[:CHUNK, :]).astype(BF16)
                oi_f[hd, rows, :] = r[CHUNK:, :]
            emit_merge_gates(1)
            for hd in heads:
                padded = [zeros_chunk, zeros_chunk]
                padded[c] = vn_b[hd, rows, :]
                upd = _dot(kdt_b[hd, sp], jnp.concatenate(padded, axis=0))
                state_ref[hd] = state_ref[hd] * eg[:, hd:hd + 1] + upd

        trows = slice(sp * SPAN, (sp + 1) * SPAN)
        for hd in heads:
            o = oi_f[hd] + _dot(aqk_b[hd, sp], vn_b[hd])
            cols = slice(hd * HEAD_DIM, (hd + 1) * HEAD_DIM)
            y = _rms(o, dnw_ref[...]) * zs_buf[trows, cols]
            yb_ref[trows, cols] = y.astype(BF16)
    emit_merge_gates(len(merge_gate_blocks))


def _delta(x, pre_norm_w, w_in_t, conv_w, a_log, dt_bias, dn_norm_w):
    b, s, d = x.shape
    tm = TILE_DELTA
    n_span = tm // SPAN
    const2 = lambda i, j: (0, 0)
    tile = lambda i, j: (i, j, 0)
    return pl.pallas_call(
        _delta_kernel,
        grid=(b, s // tm),
        in_specs=[
            pl.BlockSpec((None, tm, d), tile),
            pl.BlockSpec((1, d), const2),
            _w_in_rows(OFF_Q, OFF_A - OFF_Q),
            _w_in_rows(OFF_A, LANES),
            _w_in_rows(OFF_ZD, OFF_QM - OFF_ZD),
            _w_in_rows(OFF_GATE, N_BRANCH * D_MODEL),
            pl.BlockSpec(conv_w.shape, const2),
            pl.BlockSpec((1, LANES), const2),
            pl.BlockSpec((1, LANES), const2),
            pl.BlockSpec((1, HEAD_DIM), const2),
        ],
        out_specs=[
            pl.BlockSpec((None, tm, DN_WIDTH), tile),
            pl.BlockSpec((None, tm, N_BRANCH * D_MODEL), tile),
        ],
        out_shape=[
            jax.ShapeDtypeStruct((b, s, DN_WIDTH), BF16),
            jax.ShapeDtypeStruct((b, s, N_BRANCH * D_MODEL), BF16),
        ],
        scratch_shapes=[
            pltpu.VMEM((CONV_HALO, 3 * DN_WIDTH), F32),
            pltpu.VMEM((DN_HEADS, HEAD_DIM, HEAD_DIM), F32),
            pltpu.VMEM((tm, DN_WIDTH), F32),
            pltpu.VMEM((tm, LANES), F32),
            pltpu.VMEM((RAW_SLOTS, CONV_HALO + tm, GROUP_W), F32),
            pltpu.VMEM((DN_HEADS, n_span, 2 * SPAN, HEAD_DIM), BF16),
            pltpu.VMEM((DN_HEADS, tm, HEAD_DIM), BF16),
            pltpu.VMEM((DN_HEADS, tm, 2 * HEAD_DIM), BF16),
            pltpu.VMEM((DN_HEADS, tm // CHUNK, SPAN, HEAD_DIM), BF16),
            pltpu.VMEM((DN_HEADS, n_span, HEAD_DIM, SPAN), BF16),
            pltpu.VMEM((DN_HEADS, n_span, SPAN, SPAN), BF16),
            pltpu.VMEM((DN_HEADS, tm, HEAD_DIM), F32),
            pltpu.VMEM((2 * HEAD_GROUP, n_span, 2 * SPAN, SPAN), BF16),
            pltpu.VMEM((2 * HEAD_GROUP, n_span, SPAN, SPAN), F32),
            pltpu.VMEM((2 * HEAD_GROUP, n_span, SPAN, SPAN), F32),
            pltpu.VMEM((DN_HEADS, SPAN, HEAD_DIM), BF16),
            pltpu.VMEM((DN_HEADS, SPAN, HEAD_DIM), F32),
        ],
        compiler_params=pltpu.CompilerParams(
            dimension_semantics=("arbitrary", "arbitrary"), vmem_limit_bytes=VMEM_LIMIT_BYTES),
        name="delta_branch",
    )(x, pre_norm_w, w_in_t, w_in_t, w_in_t, w_in_t, conv_w, a_log, dt_bias, dn_norm_w)


def _merge_kernel(x_ref, gates_ref, ya_ref, yb_ref, yc_ref, wpa_ref, wpb_ref, wpc_ref,
                  wout_ref, pw_ref, o_ref):
    y = None
    for br, (y_ref, w_ref) in enumerate(((ya_ref, wpa_ref), (yb_ref, wpb_ref), (yc_ref, wpc_ref))):
        gate = gates_ref[:, br * D_MODEL:(br + 1) * D_MODEL].astype(F32)
        term = gate * _dot(y_ref[...], w_ref[...])
        y = term if y is None else y + term
    out = _dot(y.astype(BF16), wout_ref[...])
    o_ref[...] = x_ref[...] + _rms(out, pw_ref[...])


def _merge(x2, gates, ya, yb, yc, w_pa, w_pb, w_pc, w_out, post_norm_w):
    n, d = x2.shape
    tm = TILE_MERGE
    const = lambda i: (0, 0)
    row = lambda i: (i, 0)
    return pl.pallas_call(
        _merge_kernel,
        grid=(n // tm,),
        in_specs=[
            pl.BlockSpec((tm, d), row),
            pl.BlockSpec((tm, N_BRANCH * d), row),
            pl.BlockSpec((tm, POOL_WIDTH), row),
            pl.BlockSpec((tm, DN_WIDTH), row),
            pl.BlockSpec((tm, MEM_WIDTH), row),
            pl.BlockSpec(w_pa.shape, const),
            pl.BlockSpec(w_pb.shape, const),
            pl.BlockSpec(w_pc.shape, const),
            pl.BlockSpec(w_out.shape, const),
            pl.BlockSpec((1, d), const),
        ],
        out_specs=pl.BlockSpec((tm, d), row),
        out_shape=jax.ShapeDtypeStruct((n, d), F32),
        compiler_params=pltpu.CompilerParams(
            dimension_semantics=("arbitrary",), vmem_limit_bytes=VMEM_LIMIT_BYTES),
        name="merge_out",
    )(x2, gates, ya, yb, yc, w_pa, w_pb, w_pc, w_out, post_norm_w)


def _layer(x, mem, pre_norm_w, mem_norm_w, w_in, conv_w, a_log, dt_bias, dn_norm_w, pool_mix_w,
           pool_scale, w_mem_kv, w_proj_pool, w_proj_delta, w_proj_mem, w_out, post_norm_w):
    b, s, d = x.shape
    assert d == D_MODEL and s % TILE_LOCAL == 0 and s % TILE_DELTA == 0
    assert (b * s) % TILE_MERGE == 0 and TILE_DELTA % SPAN == 0
    row = lambda v: v.reshape(1, -1).astype(F32)
    pad_lanes = lambda v: jnp.pad(v.astype(F32), (0, LANES - v.shape[0])).reshape(1, LANES)

    w_in_t = w_in.T.astype(BF16)

    kt, v = _mem_kv(mem, row(mem_norm_w), w_mem_kv.astype(BF16))
    ya, yc = _local(x, row(pre_norm_w), w_in_t, pool_mix_w.astype(BF16), row(pool_scale), kt, v)
    yb, gates = _delta(x, row(pre_norm_w), w_in_t, conv_w.astype(F32), pad_lanes(a_log),
                       pad_lanes(dt_bias), row(dn_norm_w))
    out = _merge(x.reshape(b * s, d), gates.reshape(b * s, -1), ya.reshape(b * s, -1),
                 yb.reshape(b * s, -1), yc.reshape(b * s, -1), w_proj_pool.astype(BF16),
                 w_proj_delta.astype(BF16), w_proj_mem.astype(BF16), w_out.astype(BF16),
                 row(post_norm_w))
    return out.reshape(b, s, d)


def kernel(x, mem, pre_norm_w, mem_norm_w, w_in, conv_w, a_log, dt_bias, dn_norm_w, pool_mix_w,
           pool_scale, w_mem_kv, w_proj_pool, w_proj_delta, w_proj_mem, w_out, post_norm_w):
    for l in range(pre_norm_w.shape[0]):
        x = _layer(x, mem, pre_norm_w[l], mem_norm_w[l], w_in[l], conv_w[l], a_log[l], dt_bias[l],
                   dn_norm_w[l], pool_mix_w[l], pool_scale[l], w_mem_kv[l], w_proj_pool[l],
                   w_proj_delta[l], w_proj_mem[l], w_out[l], post_norm_w[l])
    return x
```

```python
import jax
import jax.numpy as jnp
from jax import lax
from jax.experimental import pallas as pl
from jax.experimental.pallas import tpu as pltpu

F32 = jnp.float32
BF16 = jnp.bfloat16

D_MODEL = 1024
HEAD_DIM = 128
DN_HEADS = D_MODEL // HEAD_DIM
DN_WIDTH = DN_HEADS * HEAD_DIM
POOL_WINDOWS = (2, 4, 8, 16)
POOL_GROUPS = len(POOL_WINDOWS)
POOL_WIDTH = D_MODEL // 2
POOL_GROUP_DIM = POOL_WIDTH // POOL_GROUPS
MEM_HEADS = 4
MEM_WIDTH = D_MODEL // 2
MEM_HEAD_DIM = MEM_WIDTH // MEM_HEADS
CONV_WIDTH = 4
CHUNK = 64
N_BRANCH = 3
EPS = 1e-6

OFF_XA = 0
OFF_ZA = OFF_XA + POOL_WIDTH
OFF_Q = OFF_ZA + POOL_WIDTH
OFF_K = OFF_Q + DN_WIDTH
OFF_V = OFF_K + DN_WIDTH
OFF_A = OFF_V + DN_WIDTH
OFF_B = OFF_A + DN_HEADS
OFF_ZD = OFF_B + DN_HEADS
OFF_QM = OFF_ZD + DN_WIDTH
OFF_ZM = OFF_QM + MEM_WIDTH
OFF_GATE = OFF_ZM + MEM_WIDTH

LANES = 128
SUBLANES = 8
SPAN = 2 * CHUNK
POOL_HALO = 16
CONV_HALO = SUBLANES
HEAD_GROUP = 2
GROUP_W = HEAD_GROUP * HEAD_DIM
RAW_SLOTS = 4
MERGE_GATE_W = 256
VMEM_LIMIT_BYTES = 56 * 1024 * 1024

TILE_LOCAL = 512
TILE_DELTA = 512
TILE_MERGE = 1024


def _rms(x, w):
    return x * lax.rsqrt(jnp.mean(x * x, axis=-1, keepdims=True) + EPS) * w


def _sigmoid(x):
    return 1.0 / (1.0 + jnp.exp(-x))


def _silu(x):
    return x * _sigmoid(x)


def _dot(a, b):
    return jnp.dot(a, b, preferred_element_type=F32)


def _dot_nt(a, b):
    return lax.dot_general(a, b, (((1,), (1,)), ((), ())), preferred_element_type=F32)


def _mem_kv_kernel(mem_ref, nw_ref, wkv_ref, kt_ref, v_ref):
    mn = _rms(mem_ref[...], nw_ref[...]).astype(BF16)
    kv = _dot(mn, wkv_ref[...])
    kt_ref[...] = kv[:, :MEM_WIDTH].T.astype(BF16)
    v_ref[...] = kv[:, MEM_WIDTH:].astype(BF16)


def _mem_kv(mem, mem_norm_w, w_kv):
    b, m, d = mem.shape
    return pl.pallas_call(
        _mem_kv_kernel,
        grid=(b,),
        in_specs=[
            pl.BlockSpec((None, m, d), lambda i: (i, 0, 0)),
            pl.BlockSpec((1, d), lambda i: (0, 0)),
            pl.BlockSpec((d, 2 * MEM_WIDTH), lambda i: (0, 0)),
        ],
        out_specs=[
            pl.BlockSpec((None, MEM_WIDTH, m), lambda i: (i, 0, 0)),
            pl.BlockSpec((None, m, MEM_WIDTH), lambda i: (i, 0, 0)),
        ],
        out_shape=[
            jax.ShapeDtypeStruct((b, MEM_WIDTH, m), BF16),
            jax.ShapeDtypeStruct((b, m, MEM_WIDTH), BF16),
        ],
        compiler_params=pltpu.CompilerParams(
            dimension_semantics=("arbitrary",), vmem_limit_bytes=VMEM_LIMIT_BYTES),
        name="mem_kv",
    )(mem, mem_norm_w, w_kv)


def _local_kernel(x_ref, nw_ref, wpool_ref, wmem_ref, mix_ref, scale_ref, kt_ref, v_ref, ya_ref,
                  yc_ref, halo_ref):
    tm = x_ref.shape[0]
    s = pl.program_id(1)

    @pl.when(s == 0)
    def _():
        halo_ref[...] = jnp.zeros_like(halo_ref)

    h = _rms(x_ref[...], nw_ref[...]).astype(BF16)
    proj_pool = _dot_nt(h, wpool_ref[...])
    proj_mem = _dot_nt(h, wmem_ref[...])
    xa = proj_pool[:, 0:POOL_WIDTH]
    za = proj_pool[:, POOL_WIDTH:]
    qm = proj_mem[:, 0:MEM_WIDTH]
    zm = proj_mem[:, MEM_WIDTH:]

    ext = jnp.concatenate([halo_ref[...], xa], axis=0)
    halo_ref[...] = xa[tm - POOL_HALO:, :]
    pos = (s * tm + 1 + lax.broadcasted_iota(jnp.int32, (tm, 1), 0)).astype(F32)
    mixed = []
    for g, window in enumerate(POOL_WINDOWS):
        cols = slice(g * POOL_GROUP_DIM, (g + 1) * POOL_GROUP_DIM)
        acc = ext[:, cols]
        width = 1
        while width < window:
            acc = acc + pltpu.roll(acc, width, 0)
            width *= 2
        mean = acc[POOL_HALO:, :] / jnp.minimum(pos, float(window))
        p = mean - xa[:, cols]
        mixed.append(_dot(p.astype(BF16), mix_ref[g]))
    m = jnp.concatenate(mixed, axis=1) * scale_ref[...]
    ya_ref[...] = (m * _silu(za)).astype(BF16)

    heads = []
    for hh in range(MEM_HEADS):
        cols = slice(hh * MEM_HEAD_DIM, (hh + 1) * MEM_HEAD_DIM)
        sc = _dot(qm[:, cols].astype(BF16), kt_ref[cols, :]) * (MEM_HEAD_DIM ** -0.5)
        e = jnp.exp(sc - jnp.max(sc, axis=-1, keepdims=True))
        o = _dot(e.astype(BF16), v_ref[:, cols])
        heads.append(o / jnp.sum(e, axis=-1, keepdims=True))
    yc_ref[...] = (jnp.concatenate(heads, axis=1) * _silu(zm)).astype(BF16)


def _w_in_rows(start, size):
    return pl.BlockSpec((pl.Element(size), pl.Element(D_MODEL)), lambda *_: (start, 0))


def _local(x, pre_norm_w, w_in_t, mix_w, pool_scale, kt, v):
    b, s, d = x.shape
    tm = TILE_LOCAL
    m = kt.shape[-1]
    const2 = lambda i, j: (0, 0)
    return pl.pallas_call(
        _local_kernel,
        grid=(b, s // tm),
        in_specs=[
            pl.BlockSpec((None, tm, d), lambda i, j: (i, j, 0)),
            pl.BlockSpec((1, d), const2),
            _w_in_rows(OFF_XA, OFF_Q - OFF_XA),
            _w_in_rows(OFF_QM, OFF_GATE - OFF_QM),
            pl.BlockSpec(mix_w.shape, lambda i, j: (0, 0, 0)),
            pl.BlockSpec((1, POOL_WIDTH), const2),
            pl.BlockSpec((None, MEM_WIDTH, m), lambda i, j: (i, 0, 0)),
            pl.BlockSpec((None, m, MEM_WIDTH), lambda i, j: (i, 0, 0)),
        ],
        out_specs=[
            pl.BlockSpec((None, tm, POOL_WIDTH), lambda i, j: (i, j, 0)),
            pl.BlockSpec((None, tm, MEM_WIDTH), lambda i, j: (i, j, 0)),
        ],
        out_shape=[
            jax.ShapeDtypeStruct((b, s, POOL_WIDTH), BF16),
            jax.ShapeDtypeStruct((b, s, MEM_WIDTH), BF16),
        ],
        scratch_shapes=[pltpu.VMEM((POOL_HALO, POOL_WIDTH), F32)],
        compiler_params=pltpu.CompilerParams(
            dimension_semantics=("arbitrary", "arbitrary"), vmem_limit_bytes=VMEM_LIMIT_BYTES),
        name="local_branches",
    )(x, pre_norm_w, w_in_t, w_in_t, mix_w, pool_scale, kt, v)


def _delta_kernel(x_ref, nw_ref, wqkv_ref, wab_ref, wzd_ref, wg_ref, convw_ref, alog_ref, dtb_ref,
                  dnw_ref, yb_ref, gates_ref, halo_ref, state_ref, h_buf, zs_buf, e_buf, raw_buf,
                  kq_lhs, kn_b, rhs_b, wq_b, kdt_b, aqk_b, u_f, tp_b, a_b, vn_b, oi_f):
    tm = x_ref.shape[0]
    n_span = tm // SPAN
    n_chunk = tm // CHUNK
    s = pl.program_id(1)
    heads = range(DN_HEADS)
    top = slice(0, SPAN)
    bot = slice(SPAN, 2 * SPAN)

    @pl.when(s == 0)
    def _():
        halo_ref[...] = jnp.zeros_like(halo_ref)
        state_ref[...] = jnp.zeros_like(state_ref)

    for i in range(n_span):
        rows = slice(i * SPAN, (i + 1) * SPAN)
        h_buf[rows, :] = _rms(x_ref[rows, :], nw_ref[...]).astype(BF16)

    ab = _dot_nt(h_buf[...], wab_ref[...])
    sp_in = ab + dtb_ref[...]
    softplus = jnp.maximum(sp_in, 0.0) + jnp.log1p(jnp.exp(-jnp.abs(sp_in)))
    g = -jnp.exp(alog_ref[...]) * softplus
    beta = _sigmoid(ab)

    row_in_chunk = lax.broadcasted_iota(jnp.int32, (tm, LANES), 0) % CHUNK
    gc = g
    shift = 1
    while shift < CHUNK:
        gc = gc + jnp.where(row_in_chunk >= shift, pltpu.roll(gc, shift, 0), 0.0)
        shift *= 2
    e = jnp.exp(gc)
    e_buf[...] = e
    glast = jnp.concatenate(
        [jnp.broadcast_to(gc[(c + 1) * CHUNK - 1:(c + 1) * CHUNK, :], (CHUNK, LANES))
         for c in range(n_chunk)], axis=0)
    kdf = jnp.exp(glast - gc)

    ri = lax.broadcasted_iota(jnp.int32, (SPAN, SPAN), 0)
    ci = lax.broadcasted_iota(jnp.int32, (SPAN, SPAN), 1)
    same_chunk = (ri // CHUNK) == (ci // CHUNK)
    tril = same_chunk & (ri >= ci)
    stril = same_chunk & (ri > ci)
    eye = (ri == ci).astype(F32)
    gct = [gc[i * SPAN:(i + 1) * SPAN, :].T for i in range(n_span)]
    block_sizes = []
    size = 1
    while size < CHUNK:
        block_sizes.append(size)
        size *= 2
    sibling = {
        size: (((ri // (2 * size)) == (ci // (2 * size))) & ((ri // size) != (ci // size))).astype(F32)
        for size in block_sizes}
    sibling_b = {size: m.astype(BF16) for size, m in sibling.items()}

    n_raw = raw_buf.shape[0]
    next_raw_slot = [0]

    def project(which, gi):
        blk = which * (DN_HEADS // HEAD_GROUP) + gi
        slot = next_raw_slot[0]
        next_raw_slot[0] = (slot + 1) % n_raw
        bcols = slice(blk * GROUP_W, (blk + 1) * GROUP_W)
        raw = _dot_nt(h_buf[...], wqkv_ref[bcols, :])
        raw_buf[slot, 0:CONV_HALO, :] = halo_ref[:, bcols]
        raw_buf[slot, CONV_HALO:, :] = raw
        halo_ref[:, bcols] = raw[tm - CONV_HALO:, :]
        return slot, bcols

    def conv_silu(slot, bcols, i):
        w = convw_ref[:, bcols]
        acc = None
        for back in range(CONV_WIDTH):
            tap = CONV_WIDTH - 1 - back
            start = CONV_HALO - back + i * SPAN
            term = raw_buf[slot, start:start + SPAN, :] * w[tap:tap + 1, :]
            acc = term if acc is None else acc + term
        out = _silu(acc)
        return [out[:, j * HEAD_DIM:(j + 1) * HEAD_DIM] for j in range(HEAD_GROUP)]

    def prep_k(gi):
        slot, bcols = project(1, gi)
        for i in range(n_span):
            rows = slice(i * SPAN, (i + 1) * SPAN)
            for j, k in enumerate(conv_silu(slot, bcols, i)):
                hd = gi * HEAD_GROUP + j
                kn = k * lax.rsqrt(jnp.sum(k * k, axis=-1, keepdims=True) + EPS)
                kb = kn * beta[rows, DN_HEADS + hd:DN_HEADS + hd + 1]
                kn_b[hd, rows, :] = kn.astype(BF16)
                rhs_b[hd, rows, HEAD_DIM:] = (kb * e[rows, hd:hd + 1]).astype(BF16)
                kq_lhs[hd, i, top, :] = kb.astype(BF16)
                kdt_b[hd, i] = (kn * kdf[rows, hd:hd + 1]).T.astype(BF16)

    def prep_q(gi):
        slot, bcols = project(0, gi)
        for i in range(n_span):
            rows = slice(i * SPAN, (i + 1) * SPAN)
            for j, q in enumerate(conv_silu(slot, bcols, i)):
                hd = gi * HEAD_GROUP + j
                qn = q * lax.rsqrt(jnp.sum(q * q, axis=-1, keepdims=True) + EPS) * (HEAD_DIM ** -0.5)
                kq_lhs[hd, i, bot, :] = qn.astype(BF16)
                qg = (qn * e[rows, hd:hd + 1]).astype(BF16)
                for c in range(2):
                    wq_b[hd, 2 * i + c, CHUNK:, :] = qg[c * CHUNK:(c + 1) * CHUNK, :]

    def prep_v(gi):
        slot, bcols = project(2, gi)
        for i in range(n_span):
            rows = slice(i * SPAN, (i + 1) * SPAN)
            for j, v in enumerate(conv_silu(slot, bcols, i)):
                hd = gi * HEAD_GROUP + j
                rhs_b[hd, rows, 0:HEAD_DIM] = (
                    v * beta[rows, DN_HEADS + hd:DN_HEADS + hd + 1]).astype(BF16)

    def zgate_piece(blk):
        cols = slice(blk * GROUP_W, (blk + 1) * GROUP_W)
        zs_buf[:, cols] = _silu(_dot_nt(h_buf[...], wzd_ref[cols, :]))

    def prep_pieces(gi):
        return [(prep_k, gi), (prep_q, gi), (prep_v, gi)]

    merge_gate_blocks = list(range(N_BRANCH * D_MODEL // MERGE_GATE_W))

    def emit_merge_gates(count):
        for _ in range(count):
            if merge_gate_blocks:
                blk = merge_gate_blocks.pop(0)
                cols = slice(blk * MERGE_GATE_W, (blk + 1) * MERGE_GATE_W)
                gates_ref[:, cols] = _sigmoid(_dot_nt(h_buf[...], wg_ref[cols, :])).astype(BF16)

    def stage_score(items):
        for hd, i, sl in items:
            rows = slice(i * SPAN, (i + 1) * SPAN)
            diff = gc[rows, hd:hd + 1] - gct[i][hd:hd + 1, :]
            decay = jnp.where(tril, jnp.exp(jnp.where(tril, diff, 0.0)), 0.0)
            kq = _dot_nt(kq_lhs[hd, i], kn_b[hd, rows, :])
            a = jnp.where(stril, kq[top, :] * decay, 0.0)
            aqk_b[hd, i] = (kq[bot, :] * decay).astype(BF16)
            a_b[sl, i] = a.astype(BF16)
            tp_b[sl, i, top, :] = (eye - a * sibling[1]).astype(BF16)

    def stage_left(size):
        def run(items):
            for hd, i, sl in items:
                a_s = a_b[sl, i] * sibling_b[size]
                tp_b[sl, i, bot, :] = _dot(tp_b[sl, i, top, :], a_s).astype(BF16)
        return run

    def stage_right(items):
        for hd, i, sl in items:
            y = _dot(tp_b[sl, i, bot, :], tp_b[sl, i, top, :])
            tp_b[sl, i, top, :] = tp_b[sl, i, top, :] - y.astype(BF16)

    def stage_uw(items):
        for hd, i, sl in items:
            rows = slice(i * SPAN, (i + 1) * SPAN)
            uw = _dot(tp_b[sl, i, top, :], rhs_b[hd, rows, :])
            u_f[hd, rows, :] = uw[:, :HEAD_DIM]
            w = uw[:, HEAD_DIM:].astype(BF16)
            for c in range(2):
                wq_b[hd, 2 * i + c, 0:CHUNK, :] = w[c * CHUNK:(c + 1) * CHUNK, :]

    stages = [stage_score]
    for size in block_sizes[1:]:
        stages += [stage_left(size), stage_right]
    stages.append(stage_uw)

    n_slot = tp_b.shape[0]
    groups = [range(g0, g0 + HEAD_GROUP) for g0 in range(0, DN_HEADS, HEAD_GROUP)]
    for fn, arg in prep_pieces(0):
        fn(arg)
        emit_merge_gates(2)
    for gi, group in enumerate(groups):
        items = [(hd, i, hd % n_slot) for hd in group for i in range(n_span)]
        if gi + 1 < len(groups):
            fillers = prep_pieces(gi + 1)
        else:
            fillers = [(zgate_piece, blk) for blk in range(DN_WIDTH // GROUP_W)]
        every = len(stages) // len(fillers)
        for n, stage in enumerate(stages):
            stage(items)
            if n % every == 0 and n // every < len(fillers):
                fn, arg = fillers[n // every]
                fn(arg)

    zeros_chunk = jnp.zeros((CHUNK, HEAD_DIM), BF16)
    for sp in range(n_span):
        for c in range(2):
            rows = slice(c * CHUNK, (c + 1) * CHUNK)
            trows = slice(sp * SPAN + c * CHUNK, sp * SPAN + (c + 1) * CHUNK)
            last = sp * SPAN + (c + 1) * CHUNK - 1
            eg = e_buf[last:last + 1, :]
            for hd in heads:
                r = _dot(wq_b[hd, 2 * sp + c], state_ref[hd].astype(BF16))
                vn_b[hd, rows, :] = (u_f[hd, trows, :] - r[:CHUNK, :]).astype(BF16)
                oi_f[hd, rows, :] = r[CHUNK:, :]
            emit_merge_gates(1)
            for hd in heads:
                padded = [zeros_chunk, zeros_chunk]
                padded[c] = vn_b[hd, rows, :]
                upd = _dot(kdt_b[hd, sp], jnp.concatenate(padded, axis=0))
                state_ref[hd] = state_ref[hd] * eg[:, hd:hd + 1] + upd

        trows = slice(sp * SPAN, (sp + 1) * SPAN)
        for hd in heads:
            o = oi_f[hd] + _dot(aqk_b[hd, sp], vn_b[hd])
            cols = slice(hd * HEAD_DIM, (hd + 1) * HEAD_DIM)
            y = _rms(o, dnw_ref[...]) * zs_buf[trows, cols]
            yb_ref[trows, cols] = y.astype(BF16)
    emit_merge_gates(len(merge_gate_blocks))


def _delta(x, pre_norm_w, w_in_t, conv_w, a_log, dt_bias, dn_norm_w):
    b, s, d = x.shape
    tm = TILE_DELTA
    n_span = tm // SPAN
    const2 = lambda i, j: (0, 0)
    tile = lambda i, j: (i, j, 0)
    return pl.pallas_call(
        _delta_kernel,
        grid=(b, s // tm),
        in_specs=[
            pl.BlockSpec((None, tm, d), tile),
            pl.BlockSpec((1, d), const2),
            _w_in_rows(OFF_Q, OFF_A - OFF_Q),
            _w_in_rows(OFF_A, LANES),
            _w_in_rows(OFF_ZD, OFF_QM - OFF_ZD),
            _w_in_rows(OFF_GATE, N_BRANCH * D_MODEL),
            pl.BlockSpec(conv_w.shape, const2),
            pl.BlockSpec((1, LANES), const2),
            pl.BlockSpec((1, LANES), const2),
            pl.BlockSpec((1, HEAD_DIM), const2),
        ],
        out_specs=[
            pl.BlockSpec((None, tm, DN_WIDTH), tile),
            pl.BlockSpec((None, tm, N_BRANCH * D_MODEL), tile),
        ],
        out_shape=[
            jax.ShapeDtypeStruct((b, s, DN_WIDTH), BF16),
            jax.ShapeDtypeStruct((b, s, N_BRANCH * D_MODEL), BF16),
        ],
        scratch_shapes=[
            pltpu.VMEM((CONV_HALO, 3 * DN_WIDTH), F32),
            pltpu.VMEM((DN_HEADS, HEAD_DIM, HEAD_DIM), F32),
            pltpu.VMEM((tm, D_MODEL), BF16),
            pltpu.VMEM((tm, DN_WIDTH), F32),
            pltpu.VMEM((tm, LANES), F32),
            pltpu.VMEM((RAW_SLOTS, CONV_HALO + tm, GROUP_W), F32),
            pltpu.VMEM((DN_HEADS, n_span, 2 * SPAN, HEAD_DIM), BF16),
            pltpu.VMEM((DN_HEADS, tm, HEAD_DIM), BF16),
            pltpu.VMEM((DN_HEADS, tm, 2 * HEAD_DIM), BF16),
            pltpu.VMEM((DN_HEADS, tm // CHUNK, SPAN, HEAD_DIM), BF16),
            pltpu.VMEM((DN_HEADS, n_span, HEAD_DIM, SPAN), BF16),
            pltpu.VMEM((DN_HEADS, n_span, SPAN, SPAN), BF16),
            pltpu.VMEM((DN_HEADS, tm, HEAD_DIM), F32),
            pltpu.VMEM((2 * HEAD_GROUP, n_span, 2 * SPAN, SPAN), BF16),
            pltpu.VMEM((2 * HEAD_GROUP, n_span, SPAN, SPAN), BF16),
            pltpu.VMEM((DN_HEADS, SPAN, HEAD_DIM), BF16),
            pltpu.VMEM((DN_HEADS, SPAN, HEAD_DIM), F32),
        ],
        compiler_params=pltpu.CompilerParams(
            dimension_semantics=("arbitrary", "arbitrary"), vmem_limit_bytes=VMEM_LIMIT_BYTES),
        name="delta_branch",
    )(x, pre_norm_w, w_in_t, w_in_t, w_in_t, w_in_t, conv_w, a_log, dt_bias, dn_norm_w)


def _merge_kernel(x_ref, gates_ref, ya_ref, yb_ref, yc_ref, wpa_ref, wpb_ref, wpc_ref,
                  wout_ref, pw_ref, o_ref):
    y = None
    for br, (y_ref, w_ref) in enumerate(((ya_ref, wpa_ref), (yb_ref, wpb_ref), (yc_ref, wpc_ref))):
        gate = gates_ref[:, br * D_MODEL:(br + 1) * D_MODEL].astype(F32)
        term = gate * _dot(y_ref[...], w_ref[...])
        y = term if y is None else y + term
    out = _dot(y.astype(BF16), wout_ref[...])
    o_ref[...] = x_ref[...] + _rms(out, pw_ref[...])


def _merge(x2, gates, ya, yb, yc, w_pa, w_pb, w_pc, w_out, post_norm_w):
    n, d = x2.shape
    tm = TILE_MERGE
    const = lambda i: (0, 0)
    row = lambda i: (i, 0)
    return pl.pallas_call(
        _merge_kernel,
        grid=(n // tm,),
        in_specs=[
            pl.BlockSpec((tm, d), row),
            pl.BlockSpec((tm, N_BRANCH * d), row),
            pl.BlockSpec((tm, POOL_WIDTH), row),
            pl.BlockSpec((tm, DN_WIDTH), row),
            pl.BlockSpec((tm, MEM_WIDTH), row),
            pl.BlockSpec(w_pa.shape, const),
            pl.BlockSpec(w_pb.shape, const),
            pl.BlockSpec(w_pc.shape, const),
            pl.BlockSpec(w_out.shape, const),
            pl.BlockSpec((1, d), const),
        ],
        out_specs=pl.BlockSpec((tm, d), row),
        out_shape=jax.ShapeDtypeStruct((n, d), F32),
        compiler_params=pltpu.CompilerParams(
            dimension_semantics=("arbitrary",), vmem_limit_bytes=VMEM_LIMIT_BYTES),
        name="merge_out",
    )(x2, gates, ya, yb, yc, w_pa, w_pb, w_pc, w_out, post_norm_w)


def _layer(x, mem, pre_norm_w, mem_norm_w, w_in, conv_w, a_log, dt_bias, dn_norm_w, pool_mix_w,
           pool_scale, w_mem_kv, w_proj_pool, w_proj_delta, w_proj_mem, w_out, post_norm_w):
    b, s, d = x.shape
    assert d == D_MODEL and s % TILE_LOCAL == 0 and s % TILE_DELTA == 0
    assert (b * s) % TILE_MERGE == 0 and TILE_DELTA % SPAN == 0
    row = lambda v: v.reshape(1, -1).astype(F32)
    pad_lanes = lambda v: jnp.pad(v.astype(F32), (0, LANES - v.shape[0])).reshape(1, LANES)

    w_in_t = w_in.T.astype(BF16)

    kt, v = _mem_kv(mem, row(mem_norm_w), w_mem_kv.astype(BF16))
    ya, yc = _local(x, row(pre_norm_w), w_in_t, pool_mix_w.astype(BF16), row(pool_scale), kt, v)
    yb, gates = _delta(x, row(pre_norm_w), w_in_t, conv_w.astype(F32), pad_lanes(a_log),
                       pad_lanes(dt_bias), row(dn_norm_w))
    out = _merge(x.reshape(b * s, d), gates.reshape(b * s, -1), ya.reshape(b * s, -1),
                 yb.reshape(b * s, -1), yc.reshape(b * s, -1), w_proj_pool.astype(BF16),
                 w_proj_delta.astype(BF16), w_proj_mem.astype(BF16), w_out.astype(BF16),
                 row(post_norm_w))
    return out.reshape(b, s, d)


def kernel(x, mem, pre_norm_w, mem_norm_w, w_in, conv_w, a_log, dt_bias, dn_norm_w, pool_mix_w,
           pool_scale, w_mem_kv, w_proj_pool, w_proj_delta, w_proj_mem, w_out, post_norm_w):
    for l in range(pre_norm_w.shape[0]):
        x = _layer(x, mem, pre_norm_w[l], mem_norm_w[l], w_in[l], conv_w[l], a_log[l], dt_bias[l],
                   dn_norm_w[l], pool_mix_w[l], pool_scale[l], w_mem_kv[l], w_proj_pool[l],
                   w_proj_delta[l], w_proj_mem[l], w_out[l], post_norm_w[l])
    return x
```

```python
import jax
import jax.numpy as jnp
from jax import lax
from jax.experimental import pallas as pl
from jax.experimental.pallas import tpu as pltpu

F32 = jnp.float32
BF16 = jnp.bfloat16

D_MODEL = 1024
HEAD_DIM = 128
DN_HEADS = D_MODEL // HEAD_DIM
DN_WIDTH = DN_HEADS * HEAD_DIM
POOL_WINDOWS = (2, 4, 8, 16)
POOL_GROUPS = len(POOL_WINDOWS)
POOL_WIDTH = D_MODEL // 2
POOL_GROUP_DIM = POOL_WIDTH // POOL_GROUPS
MEM_HEADS = 4
MEM_WIDTH = D_MODEL // 2
MEM_HEAD_DIM = MEM_WIDTH // MEM_HEADS
CONV_WIDTH = 4
CHUNK = 64
N_BRANCH = 3
EPS = 1e-6

OFF_XA = 0
OFF_ZA = OFF_XA + POOL_WIDTH
OFF_Q = OFF_ZA + POOL_WIDTH
OFF_K = OFF_Q + DN_WIDTH
OFF_V = OFF_K + DN_WIDTH
OFF_A = OFF_V + DN_WIDTH
OFF_B = OFF_A + DN_HEADS
OFF_ZD = OFF_B + DN_HEADS
OFF_QM = OFF_ZD + DN_WIDTH
OFF_ZM = OFF_QM + MEM_WIDTH
OFF_GATE = OFF_ZM + MEM_WIDTH

LANES = 128
SUBLANES = 8
SPAN = 2 * CHUNK
POOL_HALO = 16
CONV_HALO = SUBLANES
HEAD_GROUP = 2
GROUP_W = HEAD_GROUP * HEAD_DIM
RAW_SLOTS = 4
MERGE_GATE_W = 256
VMEM_LIMIT_BYTES = 56 * 1024 * 1024

TILE_LOCAL = 1024
TILE_DELTA = 512
TILE_MERGE = 1024


def _rms(x, w):
    return x * lax.rsqrt(jnp.mean(x * x, axis=-1, keepdims=True) + EPS) * w


def _sigmoid(x):
    return 1.0 / (1.0 + jnp.exp(-x))


def _silu(x):
    return x * _sigmoid(x)


def _dot(a, b):
    return jnp.dot(a, b, preferred_element_type=F32)


def _dot_nt(a, b):
    return lax.dot_general(a, b, (((1,), (1,)), ((), ())), preferred_element_type=F32)


def _mem_kv_kernel(mem_ref, nw_ref, wkv_ref, kt_ref, v_ref):
    mn = _rms(mem_ref[...], nw_ref[...]).astype(BF16)
    kv = _dot(mn, wkv_ref[...])
    kt_ref[...] = kv[:, :MEM_WIDTH].T.astype(BF16)
    v_ref[...] = kv[:, MEM_WIDTH:].astype(BF16)


def _mem_kv(mem, mem_norm_w, w_kv):
    b, m, d = mem.shape
    return pl.pallas_call(
        _mem_kv_kernel,
        grid=(b,),
        in_specs=[
            pl.BlockSpec((None, m, d), lambda i: (i, 0, 0)),
            pl.BlockSpec((1, d), lambda i: (0, 0)),
            pl.BlockSpec((d, 2 * MEM_WIDTH), lambda i: (0, 0)),
        ],
        out_specs=[
            pl.BlockSpec((None, MEM_WIDTH, m), lambda i: (i, 0, 0)),
            pl.BlockSpec((None, m, MEM_WIDTH), lambda i: (i, 0, 0)),
        ],
        out_shape=[
            jax.ShapeDtypeStruct((b, MEM_WIDTH, m), BF16),
            jax.ShapeDtypeStruct((b, m, MEM_WIDTH), BF16),
        ],
        compiler_params=pltpu.CompilerParams(
            dimension_semantics=("arbitrary",), vmem_limit_bytes=VMEM_LIMIT_BYTES),
        name="mem_kv",
    )(mem, mem_norm_w, w_kv)


def _local_kernel(x_ref, nw_ref, wpool_ref, wmem_ref, mix_ref, scale_ref, kt_ref, v_ref, ya_ref,
                  yc_ref, halo_ref):
    tm = x_ref.shape[0]
    s = pl.program_id(1)

    @pl.when(s == 0)
    def _():
        halo_ref[...] = jnp.zeros_like(halo_ref)

    h = _rms(x_ref[...], nw_ref[...]).astype(BF16)
    proj_pool = _dot_nt(h, wpool_ref[...])
    proj_mem = _dot_nt(h, wmem_ref[...])
    xa = proj_pool[:, 0:POOL_WIDTH]
    za = proj_pool[:, POOL_WIDTH:]
    qm = proj_mem[:, 0:MEM_WIDTH]
    zm = proj_mem[:, MEM_WIDTH:]

    ext = jnp.concatenate([halo_ref[...], xa], axis=0)
    halo_ref[...] = xa[tm - POOL_HALO:, :]
    pos = (s * tm + 1 + lax.broadcasted_iota(jnp.int32, (tm, 1), 0)).astype(F32)
    mixed = []
    for g, window in enumerate(POOL_WINDOWS):
        cols = slice(g * POOL_GROUP_DIM, (g + 1) * POOL_GROUP_DIM)
        acc = ext[:, cols]
        width = 1
        while width < window:
            acc = acc + pltpu.roll(acc, width, 0)
            width *= 2
        mean = acc[POOL_HALO:, :] / jnp.minimum(pos, float(window))
        p = mean - xa[:, cols]
        mixed.append(_dot(p.astype(BF16), mix_ref[g]))
    m = jnp.concatenate(mixed, axis=1) * scale_ref[...]
    ya_ref[...] = (m * _silu(za)).astype(BF16)

    heads = []
    for hh in range(MEM_HEADS):
        cols = slice(hh * MEM_HEAD_DIM, (hh + 1) * MEM_HEAD_DIM)
        sc = _dot(qm[:, cols].astype(BF16), kt_ref[cols, :]) * (MEM_HEAD_DIM ** -0.5)
        e = jnp.exp(sc - jnp.max(sc, axis=-1, keepdims=True))
        o = _dot(e.astype(BF16), v_ref[:, cols])
        heads.append(o / jnp.sum(e, axis=-1, keepdims=True))
    yc_ref[...] = (jnp.concatenate(heads, axis=1) * _silu(zm)).astype(BF16)


def _w_in_rows(start, size):
    return pl.BlockSpec((pl.Element(size), pl.Element(D_MODEL)), lambda *_: (start, 0))


def _local(x, pre_norm_w, w_in_t, mix_w, pool_scale, kt, v):
    b, s, d = x.shape
    tm = TILE_LOCAL
    m = kt.shape[-1]
    const2 = lambda i, j: (0, 0)
    return pl.pallas_call(
        _local_kernel,
        grid=(b, s // tm),
        in_specs=[
            pl.BlockSpec((None, tm, d), lambda i, j: (i, j, 0)),
            pl.BlockSpec((1, d), const2),
            _w_in_rows(OFF_XA, OFF_Q - OFF_XA),
            _w_in_rows(OFF_QM, OFF_GATE - OFF_QM),
            pl.BlockSpec(mix_w.shape, lambda i, j: (0, 0, 0)),
            pl.BlockSpec((1, POOL_WIDTH), const2),
            pl.BlockSpec((None, MEM_WIDTH, m), lambda i, j: (i, 0, 0)),
            pl.BlockSpec((None, m, MEM_WIDTH), lambda i, j: (i, 0, 0)),
        ],
        out_specs=[
            pl.BlockSpec((None, tm, POOL_WIDTH), lambda i, j: (i, j, 0)),
            pl.BlockSpec((None, tm, MEM_WIDTH), lambda i, j: (i, j, 0)),
        ],
        out_shape=[
            jax.ShapeDtypeStruct((b, s, POOL_WIDTH), BF16),
            jax.ShapeDtypeStruct((b, s, MEM_WIDTH), BF16),
        ],
        scratch_shapes=[pltpu.VMEM((POOL_HALO, POOL_WIDTH), F32)],
        compiler_params=pltpu.CompilerParams(
            dimension_semantics=("arbitrary", "arbitrary"), vmem_limit_bytes=VMEM_LIMIT_BYTES),
        name="local_branches",
    )(x, pre_norm_w, w_in_t, w_in_t, mix_w, pool_scale, kt, v)


def _delta_kernel(x_ref, nw_ref, wqkv_ref, wab_ref, wzd_ref, wg_ref, convw_ref, alog_ref, dtb_ref,
                  dnw_ref, yb_ref, gates_ref, halo_ref, state_ref, zs_buf, e_buf, raw_buf,
                  kq_lhs, kn_b, rhs_b, wq_b, kdt_b, aqk_b, u_f, tp_b, t_f, a_f, vn_b, oi_f):
    tm = x_ref.shape[0]
    n_span = tm // SPAN
    n_chunk = tm // CHUNK
    s = pl.program_id(1)
    heads = range(DN_HEADS)
    top = slice(0, SPAN)
    bot = slice(SPAN, 2 * SPAN)

    @pl.when(s == 0)
    def _():
        halo_ref[...] = jnp.zeros_like(halo_ref)
        state_ref[...] = jnp.zeros_like(state_ref)

    h = _rms(x_ref[...], nw_ref[...]).astype(BF16)

    ab = _dot_nt(h, wab_ref[...])
    sp_in = ab + dtb_ref[...]
    softplus = jnp.maximum(sp_in, 0.0) + jnp.log1p(jnp.exp(-jnp.abs(sp_in)))
    g = -jnp.exp(alog_ref[...]) * softplus
    beta = _sigmoid(ab)

    row_in_chunk = lax.broadcasted_iota(jnp.int32, (tm, LANES), 0) % CHUNK
    gc = g
    shift = 1
    while shift < CHUNK:
        gc = gc + jnp.where(row_in_chunk >= shift, pltpu.roll(gc, shift, 0), 0.0)
        shift *= 2
    e = jnp.exp(gc)
    e_buf[...] = e
    glast = jnp.concatenate(
        [jnp.broadcast_to(gc[(c + 1) * CHUNK - 1:(c + 1) * CHUNK, :], (CHUNK, LANES))
         for c in range(n_chunk)], axis=0)
    kdf = jnp.exp(glast - gc)

    ri = lax.broadcasted_iota(jnp.int32, (SPAN, SPAN), 0)
    ci = lax.broadcasted_iota(jnp.int32, (SPAN, SPAN), 1)
    same_chunk = (ri // CHUNK) == (ci // CHUNK)
    tril = same_chunk & (ri >= ci)
    stril = same_chunk & (ri > ci)
    eye = (ri == ci).astype(F32)
    gct = [gc[i * SPAN:(i + 1) * SPAN, :].T for i in range(n_span)]
    block_sizes = []
    size = 1
    while size < CHUNK:
        block_sizes.append(size)
        size *= 2
    sibling = {
        size: (((ri // (2 * size)) == (ci // (2 * size))) & ((ri // size) != (ci // size))).astype(F32)
        for size in block_sizes}

    n_raw = raw_buf.shape[0]

    def conv_silu(which, gi):
        blk = which * (DN_HEADS // HEAD_GROUP) + gi
        slot = blk % n_raw
        bcols = slice(blk * GROUP_W, (blk + 1) * GROUP_W)
        raw = _dot_nt(h, wqkv_ref[bcols, :])
        raw_buf[slot, 0:CONV_HALO, :] = halo_ref[:, bcols]
        raw_buf[slot, CONV_HALO:, :] = raw
        halo_ref[:, bcols] = raw[tm - CONV_HALO:, :]
        w = convw_ref[:, bcols]
        acc = raw * w[CONV_WIDTH - 1:CONV_WIDTH, :]
        for back in range(1, CONV_WIDTH):
            tap = CONV_WIDTH - 1 - back
            start = CONV_HALO - back
            acc = acc + raw_buf[slot, start:start + tm, :] * w[tap:tap + 1, :]
        out = _silu(acc)
        return [out[:, j * HEAD_DIM:(j + 1) * HEAD_DIM] for j in range(HEAD_GROUP)]

    def prep_k(gi):
        for j, k in enumerate(conv_silu(1, gi)):
            hd = gi * HEAD_GROUP + j
            kn = k * lax.rsqrt(jnp.sum(k * k, axis=-1, keepdims=True) + EPS)
            kb = kn * beta[:, DN_HEADS + hd:DN_HEADS + hd + 1]
            kb_b = kb.astype(BF16)
            kd = kn * kdf[:, hd:hd + 1]
            kn_b[hd] = kn.astype(BF16)
            rhs_b[hd, :, HEAD_DIM:] = (kb * e[:, hd:hd + 1]).astype(BF16)
            for i in range(n_span):
                rows = slice(i * SPAN, (i + 1) * SPAN)
                kq_lhs[hd, i, top, :] = kb_b[rows, :]
                kdt_b[hd, i] = kd[rows, :].T.astype(BF16)

    def prep_q(gi):
        for j, q in enumerate(conv_silu(0, gi)):
            hd = gi * HEAD_GROUP + j
            qn = q * lax.rsqrt(jnp.sum(q * q, axis=-1, keepdims=True) + EPS) * (HEAD_DIM ** -0.5)
            qn_b = qn.astype(BF16)
            qg = (qn * e[:, hd:hd + 1]).astype(BF16)
            for i in range(n_span):
                kq_lhs[hd, i, bot, :] = qn_b[i * SPAN:(i + 1) * SPAN, :]
            for c in range(n_chunk):
                wq_b[hd, c, CHUNK:, :] = qg[c * CHUNK:(c + 1) * CHUNK, :]

    def prep_v(gi):
        for j, v in enumerate(conv_silu(2, gi)):
            hd = gi * HEAD_GROUP + j
            rhs_b[hd, :, 0:HEAD_DIM] = (v * beta[:, DN_HEADS + hd:DN_HEADS + hd + 1]).astype(BF16)

    def zgate_piece(blk):
        cols = slice(blk * GROUP_W, (blk + 1) * GROUP_W)
        zs_buf[:, cols] = _silu(_dot_nt(h, wzd_ref[cols, :]))

    def prep_pieces(gi):
        return [(prep_k, gi), (prep_q, gi), (prep_v, gi)]

    merge_gate_blocks = list(range(N_BRANCH * D_MODEL // MERGE_GATE_W))

    def emit_merge_gates(count):
        for _ in range(count):
            if merge_gate_blocks:
                blk = merge_gate_blocks.pop(0)
                cols = slice(blk * MERGE_GATE_W, (blk + 1) * MERGE_GATE_W)
                gates_ref[:, cols] = _sigmoid(_dot_nt(h, wg_ref[cols, :])).astype(BF16)

    def stage_score(items):
        for hd, i, sl in items:
            rows = slice(i * SPAN, (i + 1) * SPAN)
            diff = gc[rows, hd:hd + 1] - gct[i][hd:hd + 1, :]
            decay = jnp.where(tril, jnp.exp(jnp.where(tril, diff, 0.0)), 0.0)
            kq = _dot_nt(kq_lhs[hd, i], kn_b[hd, rows, :])
            a = jnp.where(stril, kq[top, :] * decay, 0.0)
            aqk_b[hd, i] = (kq[bot, :] * decay).astype(BF16)
            a_f[sl, i] = a
            t1 = eye - a * sibling[1]
            t_f[sl, i] = t1
            tp_b[sl, i, top, :] = t1.astype(BF16)

    def stage_left(size):
        def run(items):
            for hd, i, sl in items:
                a_s = (a_f[sl, i] * sibling[size]).astype(BF16)
                tp_b[sl, i, bot, :] = _dot(tp_b[sl, i, top, :], a_s).astype(BF16)
        return run

    def stage_right(items):
        for hd, i, sl in items:
            t_new = t_f[sl, i] - _dot(tp_b[sl, i, bot, :], tp_b[sl, i, top, :])
            t_f[sl, i] = t_new
            tp_b[sl, i, top, :] = t_new.astype(BF16)

    def stage_uw(items):
        for hd, i, sl in items:
            rows = slice(i * SPAN, (i + 1) * SPAN)
            uw = _dot(tp_b[sl, i, top, :], rhs_b[hd, rows, :])
            u_f[hd, rows, :] = uw[:, :HEAD_DIM]
            w = uw[:, HEAD_DIM:].astype(BF16)
            for c in range(2):
                wq_b[hd, 2 * i + c, 0:CHUNK, :] = w[c * CHUNK:(c + 1) * CHUNK, :]

    stages = [stage_score]
    for size in block_sizes[1:]:
        stages += [stage_left(size), stage_right]
    stages.append(stage_uw)

    n_slot = tp_b.shape[0]
    groups = [range(g0, g0 + HEAD_GROUP) for g0 in range(0, DN_HEADS, HEAD_GROUP)]
    for fn, arg in prep_pieces(0):
        fn(arg)
        emit_merge_gates(2)
    for gi, group in enumerate(groups):
        items = [(hd, i, hd % n_slot) for hd in group for i in range(n_span)]
        if gi + 1 < len(groups):
            fillers = prep_pieces(gi + 1)
        else:
            fillers = [(zgate_piece, blk) for blk in range(DN_WIDTH // GROUP_W)]
        every = len(stages) // len(fillers)
        for n, stage in enumerate(stages):
            stage(items)
            if n % every == 0 and n // every < len(fillers):
                fn, arg = fillers[n // every]
                fn(arg)

    zeros_chunk = jnp.zeros((CHUNK, HEAD_DIM), BF16)
    for sp in range(n_span):
        for c in range(2):
            rows = slice(c * CHUNK, (c + 1) * CHUNK)
            trows = slice(sp * SPAN + c * CHUNK, sp * SPAN + (c + 1) * CHUNK)
            last = sp * SPAN + (c + 1) * CHUNK - 1
            eg = e_buf[last:last + 1, :]
            for hd in heads:
                r = _dot(wq_b[hd, 2 * sp + c], state_ref[hd].astype(BF16))
                vn_b[hd, rows, :] = (u_f[hd, trows, :] - r[:CHUNK, :]).astype(BF16)
                oi_f[hd, rows, :] = r[CHUNK:, :]
            emit_merge_gates(1)
            for hd in heads:
                padded = [zeros_chunk, zeros_chunk]
                padded[c] = vn_b[hd, rows, :]
                upd = _dot(kdt_b[hd, sp], jnp.concatenate(padded, axis=0))
                state_ref[hd] = state_ref[hd] * eg[:, hd:hd + 1] + upd

        trows = slice(sp * SPAN, (sp + 1) * SPAN)
        for hd in heads:
            o = oi_f[hd] + _dot(aqk_b[hd, sp], vn_b[hd])
            cols = slice(hd * HEAD_DIM, (hd + 1) * HEAD_DIM)
            y = _rms(o, dnw_ref[...]) * zs_buf[trows, cols]
            yb_ref[trows, cols] = y.astype(BF16)
    emit_merge_gates(len(merge_gate_blocks))


def _delta(x, pre_norm_w, w_in_t, conv_w, a_log, dt_bias, dn_norm_w):
    b, s, d = x.shape
    tm = TILE_DELTA
    n_span = tm // SPAN
    const2 = lambda i, j: (0, 0)
    tile = lambda i, j: (i, j, 0)
    return pl.pallas_call(
        _delta_kernel,
        grid=(b, s // tm),
        in_specs=[
            pl.BlockSpec((None, tm, d), tile),
            pl.BlockSpec((1, d), const2),
            _w_in_rows(OFF_Q, OFF_A - OFF_Q),
            _w_in_rows(OFF_A, LANES),
            _w_in_rows(OFF_ZD, OFF_QM - OFF_ZD),
            _w_in_rows(OFF_GATE, N_BRANCH * D_MODEL),
            pl.BlockSpec(conv_w.shape, const2),
            pl.BlockSpec((1, LANES), const2),
            pl.BlockSpec((1, LANES), const2),
            pl.BlockSpec((1, HEAD_DIM), const2),
        ],
        out_specs=[
            pl.BlockSpec((None, tm, DN_WIDTH), tile),
            pl.BlockSpec((None, tm, N_BRANCH * D_MODEL), tile),
        ],
        out_shape=[
            jax.ShapeDtypeStruct((b, s, DN_WIDTH), BF16),
            jax.ShapeDtypeStruct((b, s, N_BRANCH * D_MODEL), BF16),
        ],
        scratch_shapes=[
            pltpu.VMEM((CONV_HALO, 3 * DN_WIDTH), F32),
            pltpu.VMEM((DN_HEADS, HEAD_DIM, HEAD_DIM), F32),
            pltpu.VMEM((tm, DN_WIDTH), F32),
            pltpu.VMEM((tm, LANES), F32),
            pltpu.VMEM((RAW_SLOTS, CONV_HALO + tm, GROUP_W), F32),
            pltpu.VMEM((DN_HEADS, n_span, 2 * SPAN, HEAD_DIM), BF16),
            pltpu.VMEM((DN_HEADS, tm, HEAD_DIM), BF16),
            pltpu.VMEM((DN_HEADS, tm, 2 * HEAD_DIM), BF16),
            pltpu.VMEM((DN_HEADS, tm // CHUNK, SPAN, HEAD_DIM), BF16),
            pltpu.VMEM((DN_HEADS, n_span, HEAD_DIM, SPAN), BF16),
            pltpu.VMEM((DN_HEADS, n_span, SPAN, SPAN), BF16),
            pltpu.VMEM((DN_HEADS, tm, HEAD_DIM), F32),
            pltpu.VMEM((2 * HEAD_GROUP, n_span, 2 * SPAN, SPAN), BF16),
            pltpu.VMEM((2 * HEAD_GROUP, n_span, SPAN, SPAN), F32),
            pltpu.VMEM((2 * HEAD_GROUP, n_span, SPAN, SPAN), F32),
            pltpu.VMEM((DN_HEADS, SPAN, HEAD_DIM), BF16),
            pltpu.VMEM((DN_HEADS, SPAN, HEAD_DIM), F32),
        ],
        compiler_params=pltpu.CompilerParams(
            dimension_semantics=("arbitrary", "arbitrary"), vmem_limit_bytes=VMEM_LIMIT_BYTES),
        name="delta_branch",
    )(x, pre_norm_w, w_in_t, w_in_t, w_in_t, w_in_t, conv_w, a_log, dt_bias, dn_norm_w)


def _merge_kernel(x_ref, gates_ref, ya_ref, yb_ref, yc_ref, wpa_ref, wpb_ref, wpc_ref,
                  wout_ref, pw_ref, o_ref):
    y = None
    for br, (y_ref, w_ref) in enumerate(((ya_ref, wpa_ref), (yb_ref, wpb_ref), (yc_ref, wpc_ref))):
        gate = gates_ref[:, br * D_MODEL:(br + 1) * D_MODEL].astype(F32)
        term = gate * _dot(y_ref[...], w_ref[...])
        y = term if y is None else y + term
    out = _dot(y.astype(BF16), wout_ref[...])
    o_ref[...] = x_ref[...] + _rms(out, pw_ref[...])


def _merge(x2, gates, ya, yb, yc, w_pa, w_pb, w_pc, w_out, post_norm_w):
    n, d = x2.shape
    tm = TILE_MERGE
    const = lambda i: (0, 0)
    row = lambda i: (i, 0)
    return pl.pallas_call(
        _merge_kernel,
        grid=(n // tm,),
        in_specs=[
            pl.BlockSpec((tm, d), row),
            pl.BlockSpec((tm, N_BRANCH * d), row),
            pl.BlockSpec((tm, POOL_WIDTH), row),
            pl.BlockSpec((tm, DN_WIDTH), row),
            pl.BlockSpec((tm, MEM_WIDTH), row),
            pl.BlockSpec(w_pa.shape, const),
            pl.BlockSpec(w_pb.shape, const),
            pl.BlockSpec(w_pc.shape, const),
            pl.BlockSpec(w_out.shape, const),
            pl.BlockSpec((1, d), const),
        ],
        out_specs=pl.BlockSpec((tm, d), row),
        out_shape=jax.ShapeDtypeStruct((n, d), F32),
        compiler_params=pltpu.CompilerParams(
            dimension_semantics=("arbitrary",), vmem_limit_bytes=VMEM_LIMIT_BYTES),
        name="merge_out",
    )(x2, gates, ya, yb, yc, w_pa, w_pb, w_pc, w_out, post_norm_w)


def _layer(x, mem, pre_norm_w, mem_norm_w, w_in, conv_w, a_log, dt_bias, dn_norm_w, pool_mix_w,
           pool_scale, w_mem_kv, w_proj_pool, w_proj_delta, w_proj_mem, w_out, post_norm_w):
    b, s, d = x.shape
    assert d == D_MODEL and s % TILE_LOCAL == 0 and s % TILE_DELTA == 0
    assert (b * s) % TILE_MERGE == 0 and TILE_DELTA % SPAN == 0
    row = lambda v: v.reshape(1, -1).astype(F32)
    pad_lanes = lambda v: jnp.pad(v.astype(F32), (0, LANES - v.shape[0])).reshape(1, LANES)

    w_in_t = w_in.T.astype(BF16)

    kt, v = _mem_kv(mem, row(mem_norm_w), w_mem_kv.astype(BF16))
    ya, yc = _local(x, row(pre_norm_w), w_in_t, pool_mix_w.astype(BF16), row(pool_scale), kt, v)
    yb, gates = _delta(x, row(pre_norm_w), w_in_t, conv_w.astype(F32), pad_lanes(a_log),
                       pad_lanes(dt_bias), row(dn_norm_w))
    out = _merge(x.reshape(b * s, d), gates.reshape(b * s, -1), ya.reshape(b * s, -1),
                 yb.reshape(b * s, -1), yc.reshape(b * s, -1), w_proj_pool.astype(BF16),
                 w_proj_delta.astype(BF16), w_proj_mem.astype(BF16), w_out.astype(BF16),
                 row(post_norm_w))
    return out.reshape(b, s, d)


def kernel(x, mem, pre_norm_w, mem_norm_w, w_in, conv_w, a_log, dt_bias, dn_norm_w, pool_mix_w,
           pool_scale, w_mem_kv, w_proj_pool, w_proj_delta, w_proj_mem, w_out, post_norm_w):
    for l in range(pre_norm_w.shape[0]):
        x = _layer(x, mem, pre_norm_w[l], mem_norm_w[l], w_in[l], conv_w[l], a_log[l], dt_bias[l],
                   dn_norm_w[l], pool_mix_w[l], pool_scale[l], w_mem_kv[l], w_proj_pool[l],
                   w_proj_delta[l], w_proj_mem[l], w_out[l], post_norm_w[l])
    return x
```

```python
import jax
import jax.numpy as jnp
from jax import lax
from jax.experimental import pallas as pl
from jax.experimental.pallas import tpu as pltpu

F32 = jnp.float32
BF16 = jnp.bfloat16

D_MODEL = 1024
HEAD_DIM = 128
DN_HEADS = D_MODEL // HEAD_DIM
DN_WIDTH = DN_HEADS * HEAD_DIM
POOL_WINDOWS = (2, 4, 8, 16)
POOL_GROUPS = len(POOL_WINDOWS)
POOL_WIDTH = D_MODEL // 2
POOL_GROUP_DIM = POOL_WIDTH // POOL_GROUPS
MEM_HEADS = 4
MEM_WIDTH = D_MODEL // 2
MEM_HEAD_DIM = MEM_WIDTH // MEM_HEADS
CONV_WIDTH = 4
CHUNK = 128
N_BRANCH = 3
EPS = 1e-6

OFF_XA = 0
OFF_ZA = OFF_XA + POOL_WIDTH
OFF_Q = OFF_ZA + POOL_WIDTH
OFF_K = OFF_Q + DN_WIDTH
OFF_V = OFF_K + DN_WIDTH
OFF_A = OFF_V + DN_WIDTH
OFF_B = OFF_A + DN_HEADS
OFF_ZD = OFF_B + DN_HEADS
OFF_QM = OFF_ZD + DN_WIDTH
OFF_ZM = OFF_QM + MEM_WIDTH
OFF_GATE = OFF_ZM + MEM_WIDTH

LANES = 128
SUBLANES = 8
SPAN = LANES
CHUNKS_PER_SPAN = SPAN // CHUNK
POOL_HALO = 16
CONV_HALO = SUBLANES
HEAD_GROUP = 2
GROUP_W = HEAD_GROUP * HEAD_DIM
RAW_SLOTS = 4
MERGE_GATE_W = 256
VMEM_LIMIT_BYTES = 56 * 1024 * 1024

TILE_LOCAL = 1024
TILE_DELTA = 512
TILE_MERGE = 1024


def _rms(x, w):
    return x * lax.rsqrt(jnp.mean(x * x, axis=-1, keepdims=True) + EPS) * w


def _sigmoid(x):
    return 1.0 / (1.0 + jnp.exp(-x))


def _silu(x):
    return x * _sigmoid(x)


def _dot(a, b):
    return jnp.dot(a, b, preferred_element_type=F32)


def _dot_nt(a, b):
    return lax.dot_general(a, b, (((1,), (1,)), ((), ())), preferred_element_type=F32)


def _mem_kv_kernel(mem_ref, nw_ref, wkv_ref, kt_ref, v_ref):
    mn = _rms(mem_ref[...], nw_ref[...]).astype(BF16)
    kv = _dot(mn, wkv_ref[...])
    kt_ref[...] = kv[:, :MEM_WIDTH].T.astype(BF16)
    v_ref[...] = kv[:, MEM_WIDTH:].astype(BF16)


def _mem_kv(mem, mem_norm_w, w_kv):
    b, m, d = mem.shape
    return pl.pallas_call(
        _mem_kv_kernel,
        grid=(b,),
        in_specs=[
            pl.BlockSpec((None, m, d), lambda i: (i, 0, 0)),
            pl.BlockSpec((1, d), lambda i: (0, 0)),
            pl.BlockSpec((d, 2 * MEM_WIDTH), lambda i: (0, 0)),
        ],
        out_specs=[
            pl.BlockSpec((None, MEM_WIDTH, m), lambda i: (i, 0, 0)),
            pl.BlockSpec((None, m, MEM_WIDTH), lambda i: (i, 0, 0)),
        ],
        out_shape=[
            jax.ShapeDtypeStruct((b, MEM_WIDTH, m), BF16),
            jax.ShapeDtypeStruct((b, m, MEM_WIDTH), BF16),
        ],
        compiler_params=pltpu.CompilerParams(
            dimension_semantics=("arbitrary",), vmem_limit_bytes=VMEM_LIMIT_BYTES),
        name="mem_kv",
    )(mem, mem_norm_w, w_kv)


def _local_kernel(x_ref, nw_ref, wpool_ref, wmem_ref, mix_ref, scale_ref, kt_ref, v_ref, ya_ref,
                  yc_ref, halo_ref):
    tm = x_ref.shape[0]
    s = pl.program_id(1)

    @pl.when(s == 0)
    def _():
        halo_ref[...] = jnp.zeros_like(halo_ref)

    h = _rms(x_ref[...], nw_ref[...]).astype(BF16)
    proj_pool = _dot_nt(h, wpool_ref[...])
    proj_mem = _dot_nt(h, wmem_ref[...])
    xa = proj_pool[:, 0:POOL_WIDTH]
    za = proj_pool[:, POOL_WIDTH:]
    qm = proj_mem[:, 0:MEM_WIDTH]
    zm = proj_mem[:, MEM_WIDTH:]

    ext = jnp.concatenate([halo_ref[...], xa], axis=0)
    halo_ref[...] = xa[tm - POOL_HALO:, :]
    pos = (s * tm + 1 + lax.broadcasted_iota(jnp.int32, (tm, 1), 0)).astype(F32)
    mixed = []
    for g, window in enumerate(POOL_WINDOWS):
        cols = slice(g * POOL_GROUP_DIM, (g + 1) * POOL_GROUP_DIM)
        acc = ext[:, cols]
        width = 1
        while width < window:
            acc = acc + pltpu.roll(acc, width, 0)
            width *= 2
        mean = acc[POOL_HALO:, :] / jnp.minimum(pos, float(window))
        p = mean - xa[:, cols]
        mixed.append(_dot(p.astype(BF16), mix_ref[g]))
    m = jnp.concatenate(mixed, axis=1) * scale_ref[...]
    ya_ref[...] = (m * _silu(za)).astype(BF16)

    heads = []
    for hh in range(MEM_HEADS):
        cols = slice(hh * MEM_HEAD_DIM, (hh + 1) * MEM_HEAD_DIM)
        sc = _dot(qm[:, cols].astype(BF16), kt_ref[cols, :]) * (MEM_HEAD_DIM ** -0.5)
        e = jnp.exp(sc - jnp.max(sc, axis=-1, keepdims=True))
        o = _dot(e.astype(BF16), v_ref[:, cols])
        heads.append(o / jnp.sum(e, axis=-1, keepdims=True))
    yc_ref[...] = (jnp.concatenate(heads, axis=1) * _silu(zm)).astype(BF16)


def _w_in_rows(start, size):
    return pl.BlockSpec((pl.Element(size), pl.Element(D_MODEL)), lambda *_: (start, 0))


def _local(x, pre_norm_w, w_in_t, mix_w, pool_scale, kt, v):
    b, s, d = x.shape
    tm = TILE_LOCAL
    m = kt.shape[-1]
    const2 = lambda i, j: (0, 0)
    return pl.pallas_call(
        _local_kernel,
        grid=(b, s // tm),
        in_specs=[
            pl.BlockSpec((None, tm, d), lambda i, j: (i, j, 0)),
            pl.BlockSpec((1, d), const2),
            _w_in_rows(OFF_XA, OFF_Q - OFF_XA),
            _w_in_rows(OFF_QM, OFF_GATE - OFF_QM),
            pl.BlockSpec(mix_w.shape, lambda i, j: (0, 0, 0)),
            pl.BlockSpec((1, POOL_WIDTH), const2),
            pl.BlockSpec((None, MEM_WIDTH, m), lambda i, j: (i, 0, 0)),
            pl.BlockSpec((None, m, MEM_WIDTH), lambda i, j: (i, 0, 0)),
        ],
        out_specs=[
            pl.BlockSpec((None, tm, POOL_WIDTH), lambda i, j: (i, j, 0)),
            pl.BlockSpec((None, tm, MEM_WIDTH), lambda i, j: (i, j, 0)),
        ],
        out_shape=[
            jax.ShapeDtypeStruct((b, s, POOL_WIDTH), BF16),
            jax.ShapeDtypeStruct((b, s, MEM_WIDTH), BF16),
        ],
        scratch_shapes=[pltpu.VMEM((POOL_HALO, POOL_WIDTH), F32)],
        compiler_params=pltpu.CompilerParams(
            dimension_semantics=("arbitrary", "arbitrary"), vmem_limit_bytes=VMEM_LIMIT_BYTES),
        name="local_branches",
    )(x, pre_norm_w, w_in_t, w_in_t, mix_w, pool_scale, kt, v)


def _delta_kernel(x_ref, nw_ref, wqkv_ref, wab_ref, wzd_ref, wg_ref, convw_ref, alog_ref, dtb_ref,
                  dnw_ref, yb_ref, gates_ref, halo_ref, state_ref, zs_buf, e_buf, raw_buf,
                  kq_lhs, kn_b, rhs_b, wq_b, kdt_b, aqk_b, u_f, tp_b, t_f, a_f, vn_b, oi_f):
    tm = x_ref.shape[0]
    n_span = tm // SPAN
    n_chunk = tm // CHUNK
    s = pl.program_id(1)
    heads = range(DN_HEADS)
    top = slice(0, SPAN)
    bot = slice(SPAN, 2 * SPAN)

    @pl.when(s == 0)
    def _():
        halo_ref[...] = jnp.zeros_like(halo_ref)
        state_ref[...] = jnp.zeros_like(state_ref)

    h = _rms(x_ref[...], nw_ref[...]).astype(BF16)

    ab = _dot_nt(h, wab_ref[...])
    sp_in = ab + dtb_ref[...]
    softplus = jnp.maximum(sp_in, 0.0) + jnp.log1p(jnp.exp(-jnp.abs(sp_in)))
    g = -jnp.exp(alog_ref[...]) * softplus
    beta = _sigmoid(ab)

    row_in_chunk = lax.broadcasted_iota(jnp.int32, (tm, LANES), 0) % CHUNK
    gc = g
    shift = 1
    while shift < CHUNK:
        gc = gc + jnp.where(row_in_chunk >= shift, pltpu.roll(gc, shift, 0), 0.0)
        shift *= 2
    e = jnp.exp(gc)
    e_buf[...] = e
    glast = jnp.concatenate(
        [jnp.broadcast_to(gc[(c + 1) * CHUNK - 1:(c + 1) * CHUNK, :], (CHUNK, LANES))
         for c in range(n_chunk)], axis=0)
    kdf = jnp.exp(glast - gc)

    ri = lax.broadcasted_iota(jnp.int32, (SPAN, SPAN), 0)
    ci = lax.broadcasted_iota(jnp.int32, (SPAN, SPAN), 1)
    same_chunk = (ri // CHUNK) == (ci // CHUNK)
    tril = same_chunk & (ri >= ci)
    stril = same_chunk & (ri > ci)
    eye = (ri == ci).astype(F32)
    gct = [gc[i * SPAN:(i + 1) * SPAN, :].T for i in range(n_span)]
    block_sizes = []
    size = 1
    while size < CHUNK:
        block_sizes.append(size)
        size *= 2
    sibling = {
        size: (((ri // (2 * size)) == (ci // (2 * size))) & ((ri // size) != (ci // size))).astype(F32)
        for size in block_sizes}

    n_raw = raw_buf.shape[0]

    def conv_silu(which, gi):
        blk = which * (DN_HEADS // HEAD_GROUP) + gi
        slot = blk % n_raw
        bcols = slice(blk * GROUP_W, (blk + 1) * GROUP_W)
        raw = _dot_nt(h, wqkv_ref[bcols, :])
        raw_buf[slot, 0:CONV_HALO, :] = halo_ref[:, bcols]
        raw_buf[slot, CONV_HALO:, :] = raw
        halo_ref[:, bcols] = raw[tm - CONV_HALO:, :]
        w = convw_ref[:, bcols]
        acc = raw * w[CONV_WIDTH - 1:CONV_WIDTH, :]
        for back in range(1, CONV_WIDTH):
            tap = CONV_WIDTH - 1 - back
            start = CONV_HALO - back
            acc = acc + raw_buf[slot, start:start + tm, :] * w[tap:tap + 1, :]
        out = _silu(acc)
        return [out[:, j * HEAD_DIM:(j + 1) * HEAD_DIM] for j in range(HEAD_GROUP)]

    def prep_k(gi):
        for j, k in enumerate(conv_silu(1, gi)):
            hd = gi * HEAD_GROUP + j
            kn = k * lax.rsqrt(jnp.sum(k * k, axis=-1, keepdims=True) + EPS)
            kb = kn * beta[:, DN_HEADS + hd:DN_HEADS + hd + 1]
            kb_b = kb.astype(BF16)
            kd = kn * kdf[:, hd:hd + 1]
            kn_b[hd] = kn.astype(BF16)
            rhs_b[hd, :, HEAD_DIM:] = (kb * e[:, hd:hd + 1]).astype(BF16)
            for i in range(n_span):
                rows = slice(i * SPAN, (i + 1) * SPAN)
                kq_lhs[hd, i, top, :] = kb_b[rows, :]
                kdt_b[hd, i] = kd[rows, :].T.astype(BF16)

    def prep_q(gi):
        for j, q in enumerate(conv_silu(0, gi)):
            hd = gi * HEAD_GROUP + j
            qn = q * lax.rsqrt(jnp.sum(q * q, axis=-1, keepdims=True) + EPS) * (HEAD_DIM ** -0.5)
            qn_b = qn.astype(BF16)
            qg = (qn * e[:, hd:hd + 1]).astype(BF16)
            for i in range(n_span):
                kq_lhs[hd, i, bot, :] = qn_b[i * SPAN:(i + 1) * SPAN, :]
            for c in range(n_chunk):
                wq_b[hd, c, CHUNK:, :] = qg[c * CHUNK:(c + 1) * CHUNK, :]

    def prep_v(gi):
        for j, v in enumerate(conv_silu(2, gi)):
            hd = gi * HEAD_GROUP + j
            rhs_b[hd, :, 0:HEAD_DIM] = (v * beta[:, DN_HEADS + hd:DN_HEADS + hd + 1]).astype(BF16)

    def zgate_piece(blk):
        cols = slice(blk * GROUP_W, (blk + 1) * GROUP_W)
        zs_buf[:, cols] = _silu(_dot_nt(h, wzd_ref[cols, :]))

    def prep_pieces(gi):
        return [(prep_k, gi), (prep_q, gi), (prep_v, gi)]

    merge_gate_blocks = list(range(N_BRANCH * D_MODEL // MERGE_GATE_W))

    def emit_merge_gates(count):
        for _ in range(count):
            if merge_gate_blocks:
                blk = merge_gate_blocks.pop(0)
                cols = slice(blk * MERGE_GATE_W, (blk + 1) * MERGE_GATE_W)
                gates_ref[:, cols] = _sigmoid(_dot_nt(h, wg_ref[cols, :])).astype(BF16)

    def stage_score(items):
        for hd, i, sl in items:
            rows = slice(i * SPAN, (i + 1) * SPAN)
            diff = gc[rows, hd:hd + 1] - gct[i][hd:hd + 1, :]
            decay = jnp.where(tril, jnp.exp(jnp.where(tril, diff, 0.0)), 0.0)
            kq = _dot_nt(kq_lhs[hd, i], kn_b[hd, rows, :])
            a = jnp.where(stril, kq[top, :] * decay, 0.0)
            aqk_b[hd, i] = (kq[bot, :] * decay).astype(BF16)
            a_f[sl, i] = a
            t1 = eye - a * sibling[1]
            t_f[sl, i] = t1
            tp_b[sl, i, top, :] = t1.astype(BF16)

    def stage_left(size):
        def run(items):
            for hd, i, sl in items:
                a_s = (a_f[sl, i] * sibling[size]).astype(BF16)
                tp_b[sl, i, bot, :] = _dot(tp_b[sl, i, top, :], a_s).astype(BF16)
        return run

    def stage_right(items):
        for hd, i, sl in items:
            t_new = t_f[sl, i] - _dot(tp_b[sl, i, bot, :], tp_b[sl, i, top, :])
            t_f[sl, i] = t_new
            tp_b[sl, i, top, :] = t_new.astype(BF16)

    def stage_uw(items):
        for hd, i, sl in items:
            rows = slice(i * SPAN, (i + 1) * SPAN)
            uw = _dot(tp_b[sl, i, top, :], rhs_b[hd, rows, :])
            u_f[hd, rows, :] = uw[:, :HEAD_DIM]
            w = uw[:, HEAD_DIM:].astype(BF16)
            for c in range(CHUNKS_PER_SPAN):
                wq_b[hd, CHUNKS_PER_SPAN * i + c, 0:CHUNK, :] = w[c * CHUNK:(c + 1) * CHUNK, :]

    stages = [stage_score]
    for size in block_sizes[1:]:
        stages += [stage_left(size), stage_right]
    stages.append(stage_uw)

    n_slot = tp_b.shape[0]
    groups = [range(g0, g0 + HEAD_GROUP) for g0 in range(0, DN_HEADS, HEAD_GROUP)]
    for fn, arg in prep_pieces(0):
        fn(arg)
        emit_merge_gates(2)
    for gi, group in enumerate(groups):
        items = [(hd, i, hd % n_slot) for hd in group for i in range(n_span)]
        if gi + 1 < len(groups):
            fillers = prep_pieces(gi + 1)
        else:
            fillers = [(zgate_piece, blk) for blk in range(DN_WIDTH // GROUP_W)]
        every = len(stages) // len(fillers)
        for n, stage in enumerate(stages):
            stage(items)
            if n % every == 0 and n // every < len(fillers):
                fn, arg = fillers[n // every]
                fn(arg)

    zeros_chunk = jnp.zeros((CHUNK, HEAD_DIM), BF16)
    gates_per_step = -(-len(merge_gate_blocks) // n_chunk)
    for sp in range(n_span):
        for c in range(CHUNKS_PER_SPAN):
            rows = slice(c * CHUNK, (c + 1) * CHUNK)
            trows = slice(sp * SPAN + c * CHUNK, sp * SPAN + (c + 1) * CHUNK)
            last = sp * SPAN + (c + 1) * CHUNK - 1
            eg = e_buf[last:last + 1, :]
            for hd in heads:
                r = _dot(wq_b[hd, CHUNKS_PER_SPAN * sp + c], state_ref[hd].astype(BF16))
                vn_b[hd, rows, :] = (u_f[hd, trows, :] - r[:CHUNK, :]).astype(BF16)
                oi_f[hd, rows, :] = r[CHUNK:, :]
            emit_merge_gates(gates_per_step)
            for hd in heads:
                padded = [zeros_chunk] * CHUNKS_PER_SPAN
                padded[c] = vn_b[hd, rows, :]
                upd = _dot(kdt_b[hd, sp], jnp.concatenate(padded, axis=0))
                state_ref[hd] = state_ref[hd] * eg[:, hd:hd + 1] + upd

        trows = slice(sp * SPAN, (sp + 1) * SPAN)
        for hd in heads:
            o = oi_f[hd] + _dot(aqk_b[hd, sp], vn_b[hd])
            cols = slice(hd * HEAD_DIM, (hd + 1) * HEAD_DIM)
            y = _rms(o, dnw_ref[...]) * zs_buf[trows, cols]
            yb_ref[trows, cols] = y.astype(BF16)
    emit_merge_gates(len(merge_gate_blocks))


def _delta(x, pre_norm_w, w_in_t, conv_w, a_log, dt_bias, dn_norm_w):
    b, s, d = x.shape
    tm = TILE_DELTA
    n_span = tm // SPAN
    const2 = lambda i, j: (0, 0)
    tile = lambda i, j: (i, j, 0)
    return pl.pallas_call(
        _delta_kernel,
        grid=(b, s // tm),
        in_specs=[
            pl.BlockSpec((None, tm, d), tile),
            pl.BlockSpec((1, d), const2),
            _w_in_rows(OFF_Q, OFF_A - OFF_Q),
            _w_in_rows(OFF_A, LANES),
            _w_in_rows(OFF_ZD, OFF_QM - OFF_ZD),
            _w_in_rows(OFF_GATE, N_BRANCH * D_MODEL),
            pl.BlockSpec(conv_w.shape, const2),
            pl.BlockSpec((1, LANES), const2),
            pl.BlockSpec((1, LANES), const2),
            pl.BlockSpec((1, HEAD_DIM), const2),
        ],
        out_specs=[
            pl.BlockSpec((None, tm, DN_WIDTH), tile),
            pl.BlockSpec((None, tm, N_BRANCH * D_MODEL), tile),
        ],
        out_shape=[
            jax.ShapeDtypeStruct((b, s, DN_WIDTH), BF16),
            jax.ShapeDtypeStruct((b, s, N_BRANCH * D_MODEL), BF16),
        ],
        scratch_shapes=[
            pltpu.VMEM((CONV_HALO, 3 * DN_WIDTH), F32),
            pltpu.VMEM((DN_HEADS, HEAD_DIM, HEAD_DIM), F32),
            pltpu.VMEM((tm, DN_WIDTH), F32),
            pltpu.VMEM((tm, LANES), F32),
            pltpu.VMEM((RAW_SLOTS, CONV_HALO + tm, GROUP_W), F32),
            pltpu.VMEM((DN_HEADS, n_span, 2 * SPAN, HEAD_DIM), BF16),
            pltpu.VMEM((DN_HEADS, tm, HEAD_DIM), BF16),
            pltpu.VMEM((DN_HEADS, tm, 2 * HEAD_DIM), BF16),
            pltpu.VMEM((DN_HEADS, tm // CHUNK, 2 * CHUNK, HEAD_DIM), BF16),
            pltpu.VMEM((DN_HEADS, n_span, HEAD_DIM, SPAN), BF16),
            pltpu.VMEM((DN_HEADS, n_span, SPAN, SPAN), BF16),
            pltpu.VMEM((DN_HEADS, tm, HEAD_DIM), F32),
            pltpu.VMEM((2 * HEAD_GROUP, n_span, 2 * SPAN, SPAN), BF16),
            pltpu.VMEM((2 * HEAD_GROUP, n_span, SPAN, SPAN), F32),
            pltpu.VMEM((2 * HEAD_GROUP, n_span, SPAN, SPAN), F32),
            pltpu.VMEM((DN_HEADS, SPAN, HEAD_DIM), BF16),
            pltpu.VMEM((DN_HEADS, SPAN, HEAD_DIM), F32),
        ],
        compiler_params=pltpu.CompilerParams(
            dimension_semantics=("arbitrary", "arbitrary"), vmem_limit_bytes=VMEM_LIMIT_BYTES),
        name="delta_branch",
    )(x, pre_norm_w, w_in_t, w_in_t, w_in_t, w_in_t, conv_w, a_log, dt_bias, dn_norm_w)


def _merge_kernel(x_ref, gates_ref, ya_ref, yb_ref, yc_ref, wpa_ref, wpb_ref, wpc_ref,
                  wout_ref, pw_ref, o_ref):
    y = None
    for br, (y_ref, w_ref) in enumerate(((ya_ref, wpa_ref), (yb_ref, wpb_ref), (yc_ref, wpc_ref))):
        gate = gates_ref[:, br * D_MODEL:(br + 1) * D_MODEL].astype(F32)
        term = gate * _dot(y_ref[...], w_ref[...])
        y = term if y is None else y + term
    out = _dot(y.astype(BF16), wout_ref[...])
    o_ref[...] = x_ref[...] + _rms(out, pw_ref[...])


def _merge(x2, gates, ya, yb, yc, w_pa, w_pb, w_pc, w_out, post_norm_w):
    n, d = x2.shape
    tm = TILE_MERGE
    const = lambda i: (0, 0)
    row = lambda i: (i, 0)
    return pl.pallas_call(
        _merge_kernel,
        grid=(n // tm,),
        in_specs=[
            pl.BlockSpec((tm, d), row),
            pl.BlockSpec((tm, N_BRANCH * d), row),
            pl.BlockSpec((tm, POOL_WIDTH), row),
            pl.BlockSpec((tm, DN_WIDTH), row),
            pl.BlockSpec((tm, MEM_WIDTH), row),
            pl.BlockSpec(w_pa.shape, const),
            pl.BlockSpec(w_pb.shape, const),
            pl.BlockSpec(w_pc.shape, const),
            pl.BlockSpec(w_out.shape, const),
            pl.BlockSpec((1, d), const),
        ],
        out_specs=pl.BlockSpec((tm, d), row),
        out_shape=jax.ShapeDtypeStruct((n, d), F32),
        compiler_params=pltpu.CompilerParams(
            dimension_semantics=("arbitrary",), vmem_limit_bytes=VMEM_LIMIT_BYTES),
        name="merge_out",
    )(x2, gates, ya, yb, yc, w_pa, w_pb, w_pc, w_out, post_norm_w)


def _layer(x, mem, pre_norm_w, mem_norm_w, w_in, conv_w, a_log, dt_bias, dn_norm_w, pool_mix_w,
           pool_scale, w_mem_kv, w_proj_pool, w_proj_delta, w_proj_mem, w_out, post_norm_w):
    b, s, d = x.shape
    assert d == D_MODEL and s % TILE_LOCAL == 0 and s % TILE_DELTA == 0
    assert (b * s) % TILE_MERGE == 0 and TILE_DELTA % SPAN == 0
    row = lambda v: v.reshape(1, -1).astype(F32)
    pad_lanes = lambda v: jnp.pad(v.astype(F32), (0, LANES - v.shape[0])).reshape(1, LANES)

    w_in_t = w_in.T.astype(BF16)

    kt, v = _mem_kv(mem, row(mem_norm_w), w_mem_kv.astype(BF16))
    ya, yc = _local(x, row(pre_norm_w), w_in_t, pool_mix_w.astype(BF16), row(pool_scale), kt, v)
    yb, gates = _delta(x, row(pre_norm_w), w_in_t, conv_w.astype(F32), pad_lanes(a_log),
                       pad_lanes(dt_bias), row(dn_norm_w))
    out = _merge(x.reshape(b * s, d), gates.reshape(b * s, -1), ya.reshape(b * s, -1),
                 yb.reshape(b * s, -1), yc.reshape(b * s, -1), w_proj_pool.astype(BF16),
                 w_proj_delta.astype(BF16), w_proj_mem.astype(BF16), w_out.astype(BF16),
                 row(post_norm_w))
    return out.reshape(b, s, d)


def kernel(x, mem, pre_norm_w, mem_norm_w, w_in, conv_w, a_log, dt_bias, dn_norm_w, pool_mix_w,
           pool_scale, w_mem_kv, w_proj_pool, w_proj_delta, w_proj_mem, w_out, post_norm_w):
    for l in range(pre_norm_w.shape[0]):
        x = _layer(x, mem, pre_norm_w[l], mem_norm_w[l], w_in[l], conv_w[l], a_log[l], dt_bias[l],
                   dn_norm_w[l], pool_mix_w[l], pool_scale[l], w_mem_kv[l], w_proj_pool[l],
                   w_proj_delta[l], w_proj_mem[l], w_out[l], post_norm_w[l])
    return x
```
